```python
import math
import jax
import jax.numpy as jnp
from jax import lax
import numpy as np

D_MODEL = 1024
BATCH = 8
SEQ = 4096
DEPTH = 4

CHUNK = 64

MIX_WIDTH = D_MODEL
POOL_WIDTH = MIX_WIDTH // 2
POOL_WINDOWS = (2, 4, 8, 16)
POOL_GROUPS = len(POOL_WINDOWS)
POOL_GROUP_DIM = POOL_WIDTH // POOL_GROUPS
HEAD_DIM = 64
ATTN_WIDTH = MIX_WIDTH - POOL_WIDTH
ATTN_HEADS = ATTN_WIDTH // HEAD_DIM

OFF_POOL_U = 0
OFF_POOL_G = OFF_POOL_U + POOL_WIDTH
OFF_Q = OFF_POOL_G + POOL_WIDTH
OFF_K = OFF_Q + ATTN_WIDTH
OFF_V = OFF_K + ATTN_WIDTH
OFF_ATTN_G = OFF_V + ATTN_WIDTH
OFF_F = OFF_ATTN_G + ATTN_WIDTH
IN_COLS = OFF_F + ATTN_HEADS

Q_BLOCK = 128
RMS_EPS = 1e-6
NEG_INF = -1e30

kernel_name = "hymba_pool_fox_trunk"


def rmsnorm(x, g):
    xf = x.astype(jnp.float32)
    y = xf * lax.rsqrt(jnp.mean(xf * xf, axis=-1, keepdims=True) + RMS_EPS)
    return (y * g.astype(jnp.float32)).astype(x.dtype)


def multi_scale_pool(u, pool_w, pool_scale):
    b, s, _ = u.shape
    uf = u.astype(jnp.float32)
    cs = jnp.concatenate([jnp.zeros((b, 1, POOL_WIDTH), jnp.float32), jnp.cumsum(uf, axis=1)], axis=1)
    t = jnp.arange(s)
    parts = []
    for gi, w in enumerate(POOL_WINDOWS):
        sl = slice(gi * POOL_GROUP_DIM, (gi + 1) * POOL_GROUP_DIM)
        c_g = cs[:, :, sl]
        lower = jnp.concatenate([jnp.zeros((b, w - 1, POOL_GROUP_DIM), jnp.float32), c_g[:, : s + 1 - w]], axis=1)
        count = jnp.minimum(t + 1, w).astype(jnp.float32)[None, :, None]
        parts.append((c_g[:, 1:] - lower) / count - uf[:, :, sl])
    d = jnp.stack(parts, axis=2)
    y = jnp.einsum('bsgc,gcd->bsgd', d, pool_w.astype(jnp.float32)).reshape(b, s, POOL_WIDTH)
    y = y * pool_scale.astype(jnp.float32)
    return y.astype(u.dtype)


def forgetting_attention(q, k, v, log_f):
    b, s, h, dh = q.shape
    q = q.transpose(0, 2, 1, 3)
    k = k.transpose(0, 2, 1, 3)
    v = v.transpose(0, 2, 1, 3)
    c = jnp.cumsum(log_f, axis=1).transpose(0, 2, 1)
    scale = 1.0 / math.sqrt(dh)
    outs = []
    for i in range(s // Q_BLOCK):
        q0, q1 = i * Q_BLOCK, (i + 1) * Q_BLOCK
        qb = q[:, :, q0:q1]
        kb = k[:, :, :q1]
        vb = v[:, :, :q1]
        logits = jnp.einsum('bhqd,bhkd->bhqk', qb, kb).astype(jnp.float32) * scale
        logits = logits + (c[:, :, q0:q1, None] - c[:, :, None, :q1])
        mask = jnp.arange(q0, q1)[:, None] >= jnp.arange(q1)[None, :]
        logits = jnp.where(mask[None, None], logits, NEG_INF)
        p = jax.nn.softmax(logits, axis=-1)
        outs.append(jnp.einsum('bhqk,bhkd->bhqd', p.astype(vb.dtype), vb))
    o = jnp.concatenate(outs, axis=2)
    return o.transpose(0, 2, 1, 3).reshape(b, s, h * dh)


def setup_inputs(seed: int = 0) -> dict:
    key = jax.random.key(seed)
    ks = jax.random.split(key, 9)
    x = jax.random.normal(ks[0], (BATCH, SEQ, D_MODEL), jnp.float32)
    norm_g = 1.0 + 0.02 * jax.random.normal(ks[1], (DEPTH, D_MODEL), jnp.float32)
    w_in = jax.random.normal(ks[2], (DEPTH, D_MODEL, IN_COLS), jnp.float32) * D_MODEL ** -0.5
    forget_bias = jax.random.uniform(ks[3], (DEPTH, ATTN_HEADS), jnp.float32, minval=1.0, maxval=3.0)
    pool_w = jax.random.normal(ks[4], (DEPTH, POOL_GROUPS, POOL_GROUP_DIM, POOL_GROUP_DIM), jnp.float32) * POOL_GROUP_DIM ** -0.5
    pool_scale = 1.0 + 0.02 * jax.random.normal(ks[5], (DEPTH, POOL_WIDTH), jnp.float32)
    w_out = jax.random.normal(ks[6], (DEPTH, MIX_WIDTH, D_MODEL), jnp.float32) * MIX_WIDTH ** -0.5
    final_g = 1.0 + 0.02 * jax.random.normal(ks[7], (D_MODEL,), jnp.float32)
    return {"x": x, "norm_g": norm_g, "w_in": w_in, "forget_bias": forget_bias,
            "pool_w": pool_w, "pool_scale": pool_scale, "w_out": w_out, "final_g": final_g}


def reference(x, norm_g, w_in, forget_bias, pool_w, pool_scale, w_out, final_g):
    b, s, _ = x.shape
    for layer in range(DEPTH):
        h = rmsnorm(x, norm_g[layer])
        proj = h @ w_in[layer]
        pool_u = proj[..., OFF_POOL_U:OFF_POOL_G]
        pool_g = proj[..., OFF_POOL_G:OFF_Q]
        q = proj[..., OFF_Q:OFF_K].reshape(b, s, ATTN_HEADS, HEAD_DIM)
        k = proj[..., OFF_K:OFF_V].reshape(b, s, ATTN_HEADS, HEAD_DIM)
        v = proj[..., OFF_V:OFF_ATTN_G].reshape(b, s, ATTN_HEADS, HEAD_DIM)
        attn_g = proj[..., OFF_ATTN_G:OFF_F]
        log_f = jax.nn.log_sigmoid(proj[..., OFF_F:IN_COLS].astype(jnp.float32)
                                   + forget_bias[layer].astype(jnp.float32))
        pool_out = multi_scale_pool(pool_u, pool_w[layer], pool_scale[layer]) * jax.nn.silu(pool_g)
        attn_out = forgetting_attention(q, k, v, log_f) * jax.nn.silu(attn_g)
        mixed = jnp.concatenate([pool_out, attn_out], axis=-1)
        x = x + mixed @ w_out[layer]
    return rmsnorm(x, final_g)
```

```python
import functools

import jax
import jax.numpy as jnp
from jax import lax
from jax.experimental import pallas as pl
from jax.experimental.pallas import tpu as pltpu

F32 = jnp.float32
BF16 = jnp.bfloat16

D_MODEL = 1024
DEPTH = 4
POOL_WIDTH = 512
POOL_WINDOWS = (2, 4, 8, 16)
POOL_GROUP_DIM = 128
HEAD_DIM = 64
ATTN_WIDTH = 512
ATTN_HEADS = 8
SEG = 512
N_SEG = 6
RMS_EPS = 1e-6
NEG_INF = -1e30

LANES = 128
HALO = 16
VMEM_LIMIT = 56 * 1024 * 1024

ROW_TILE = 512
Q_TILE = 512
KV_TILE = 512


def _silu(g):
    return g * (1.0 / (1.0 + jnp.exp(-g)))


def _log_sigmoid(z):
    return jnp.minimum(z, 0.0) - jnp.log1p(jnp.exp(-jnp.abs(z)))


def _proj_kernel(x_ref, g_ref, w_ref, wf_ref, fb_ref, tri_ref, ones_ref,
                 u_ref, pg_ref, q_ref, k_ref, v_ref, ag_ref, c_ref, carry_ref):
    @pl.when(pl.program_id(1) == 0)
    def _():
        carry_ref[...] = jnp.zeros_like(carry_ref)

    x = x_ref[0]
    ms = jnp.mean(x * x, axis=-1, keepdims=True)
    h = (x * lax.rsqrt(ms + RMS_EPS) * g_ref[...]).astype(BF16)

    def seg(i):
        return jnp.dot(h, w_ref[:, i * SEG:(i + 1) * SEG], preferred_element_type=F32)

    u_ref[0] = seg(0)
    pg_ref[0] = seg(1)
    q_ref[0] = (seg(2) * (HEAD_DIM ** -0.5)).astype(BF16)
    k_ref[0] = seg(3).astype(BF16)
    v_ref[0] = seg(4).astype(BF16)
    ag_ref[0] = seg(5)

    z = jnp.dot(h, wf_ref[...], preferred_element_type=F32)
    lf = _log_sigmoid(z.T[:ATTN_HEADS] + fb_ref[...])
    tm = lf.shape[1]
    carry = carry_ref[...]
    for j in range(tm // LANES):
        blk = lf[:, j * LANES:(j + 1) * LANES]
        loc = jnp.dot(blk, tri_ref[...], preferred_element_type=F32, precision=lax.Precision.HIGHEST)
        c_ref[0, :, j * LANES:(j + 1) * LANES] = loc + carry
        carry = carry + jnp.dot(blk, ones_ref[...], preferred_element_type=F32,
                                precision=lax.Precision.HIGHEST)
    carry_ref[...] = carry


def _proj(x, g, w_main, w_f, fb, tri, ones, layer):
    b, s, d = x.shape
    tm = ROW_TILE
    grid = (b, s // tm)
    row = lambda bi, si: (bi, si, 0)
    wide_f32 = jax.ShapeDtypeStruct((b, s, SEG), F32)
    wide_bf16 = jax.ShapeDtypeStruct((b, s, SEG), BF16)
    out_shape = (wide_f32, wide_f32, wide_bf16, wide_bf16, wide_bf16, wide_f32,
                 jax.ShapeDtypeStruct((b, ATTN_HEADS, s), F32))
    wide_spec = pl.BlockSpec((1, tm, SEG), row)
    return pl.pallas_call(
        _proj_kernel,
        grid=grid,
        in_specs=[
            pl.BlockSpec((1, tm, d), row),
            pl.BlockSpec((None, 1, d), lambda bi, si: (layer, 0, 0)),
            pl.BlockSpec((None, d, N_SEG * SEG), lambda bi, si: (layer, 0, 0)),
            pl.BlockSpec((None, d, LANES), lambda bi, si: (layer, 0, 0)),
            pl.BlockSpec((None, ATTN_HEADS, 1), lambda bi, si: (layer, 0, 0)),
            pl.BlockSpec((LANES, LANES), lambda bi, si: (0, 0)),
            pl.BlockSpec((LANES, LANES), lambda bi, si: (0, 0)),
        ],
        out_specs=(wide_spec,) * 6 + (pl.BlockSpec((1, ATTN_HEADS, tm), lambda bi, si: (bi, 0, si)),),
        out_shape=out_shape,
        scratch_shapes=[pltpu.VMEM((ATTN_HEADS, LANES), F32)],
        compiler_params=pltpu.CompilerParams(
            dimension_semantics=("arbitrary", "arbitrary"), vmem_limit_bytes=VMEM_LIMIT),
        name="proj",
    )(x, g, w_main, w_f, fb, tri, ones)


def _attn_kernel(q_ref, k_ref, v_ref, c_ref, ag_ref, o_ref, m_sc, l_sc, acc_sc):
    qi = pl.program_id(2)
    tq, tk = Q_TILE, KV_TILE
    lane = lax.broadcasted_iota(jnp.int32, (1, LANES), 1)
    first = lane < HEAD_DIM
    q = q_ref[0]
    zero = jnp.zeros_like(q)
    q_heads = (jnp.where(first, q, zero), jnp.where(first, zero, q))

    m_sc[...] = jnp.full_like(m_sc, NEG_INF)
    l_sc[...] = jnp.zeros_like(l_sc)
    acc_sc[...] = jnp.zeros_like(acc_sc)

    def step(j, masked):
        start = pl.multiple_of(j * tk, tk)
        ks = k_ref[0, pl.ds(start, tk), :]
        vs = v_ref[0, pl.ds(start, tk), :]
        for hh in range(2):
            s = lax.dot_general(q_heads[hh], ks, (((1,), (1,)), ((), ())),
                                preferred_element_type=F32)
            s = s - c_ref[0, 0, hh:hh + 1, pl.ds(start, tk)]
            if masked:
                r = lax.broadcasted_iota(jnp.int32, (tq, tk), 0)
                c = lax.broadcasted_iota(jnp.int32, (tq, tk), 1)
                s = jnp.where(r >= c, s, NEG_INF)
            m_old = m_sc[hh]
            m_new = jnp.maximum(m_old, jnp.max(s, axis=-1, keepdims=True))
            alpha = jnp.exp(m_old - m_new)
            p = jnp.exp(s - m_new)
            l_sc[hh] = alpha * l_sc[hh] + jnp.sum(p, axis=-1, keepdims=True)
            acc_sc[hh] = alpha * acc_sc[hh] + jnp.dot(p.astype(BF16), vs, preferred_element_type=F32)
            m_sc[hh] = m_new

    def body(j, carry):
        step(j, False)
        return carry

    lax.fori_loop(0, qi, body, 0)
    step(qi, True)

    out = jnp.where(first, acc_sc[0] / l_sc[0], acc_sc[1] / l_sc[1])
    o_ref[0] = (out * _silu(ag_ref[0])).astype(BF16)


def _attn(q, k, v, c, ag):
    b, s, _ = q.shape
    pairs = ATTN_WIDTH // LANES
    tq = Q_TILE
    grid = (b, pairs, s // tq)
    tile = pl.BlockSpec((1, tq, LANES), lambda bi, hp, qi: (bi, qi, hp))
    full = pl.BlockSpec((1, s, LANES), lambda bi, hp, qi: (bi, 0, hp))
    return pl.pallas_call(
        _attn_kernel,
        grid=grid,
        in_specs=[tile, full, full,
                  pl.BlockSpec((1, 1, 2, s), lambda bi, hp, qi: (bi, hp, 0, 0)),
                  tile],
        out_specs=tile,
        out_shape=jax.ShapeDtypeStruct((b, s, ATTN_WIDTH), BF16),
        scratch_shapes=[pltpu.VMEM((2, tq, 1), F32), pltpu.VMEM((2, tq, 1), F32),
                        pltpu.VMEM((2, tq, LANES), F32)],
        compiler_params=pltpu.CompilerParams(
            dimension_semantics=("arbitrary", "arbitrary", "arbitrary"), vmem_limit_bytes=VMEM_LIMIT),
        name="attn",
    )(q, k, v, c.reshape(b, pairs, 2, s), ag)


def _out_kernel(x_ref, u_ref, pg_ref, a_ref, pw_ref, ps_ref, wo_ref, fg_ref, o_ref, ext_ref, *, final):
    si = pl.program_id(1)
    tm = u_ref.shape[1]

    @pl.when(si == 0)
    def _():
        ext_ref[0:HALO, :] = jnp.zeros((HALO, POOL_WIDTH), F32)

    u = u_ref[0]
    ext_ref[HALO:, :] = u
    pos = si * tm + lax.broadcasted_iota(jnp.int32, (tm, 1), 0)
    parts = []
    for gi, w in enumerate(POOL_WINDOWS):
        cols = slice(gi * POOL_GROUP_DIM, (gi + 1) * POOL_GROUP_DIM)
        win = ext_ref[:, cols]
        for st in range(gi + 1):
            win = win + pltpu.roll(win, 1 << st, axis=0)
        count = jnp.minimum(pos + 1, w).astype(F32)
        dlt = win[HALO:] / count - u[:, cols]
        parts.append(jnp.dot(dlt.astype(BF16), pw_ref[gi], preferred_element_type=F32))
    ext_ref[0:HALO, :] = u[tm - HALO:]
    y = jnp.concatenate(parts, axis=1) * ps_ref[...]
    pool_out = (y * _silu(pg_ref[0])).astype(BF16)
    acc = jnp.dot(pool_out, wo_ref[0:POOL_WIDTH, :], preferred_element_type=F32)
    acc = acc + jnp.dot(a_ref[0], wo_ref[POOL_WIDTH:, :], preferred_element_type=F32)
    xn = x_ref[0] + acc
    if final:
        ms = jnp.mean(xn * xn, axis=-1, keepdims=True)
        xn = xn * lax.rsqrt(ms + RMS_EPS) * fg_ref[...]
    o_ref[0] = xn


def _out(x, u, pg, a, pool_w, pool_scale, w_out, final_g, layer, final):
    b, s, d = x.shape
    tm = ROW_TILE
    row = lambda bi, si: (bi, si, 0)
    wide = pl.BlockSpec((1, tm, SEG), row)
    return pl.pallas_call(
        functools.partial(_out_kernel, final=final),
        grid=(b, s // tm),
        in_specs=[
            pl.BlockSpec((1, tm, d), row), wide, wide, wide,
            pl.BlockSpec((None, len(POOL_WINDOWS), POOL_GROUP_DIM, POOL_GROUP_DIM),
                         lambda bi, si: (layer, 0, 0, 0)),
            pl.BlockSpec((None, 1, POOL_WIDTH), lambda bi, si: (layer, 0, 0)),
            pl.BlockSpec((None, d, d), lambda bi, si: (layer, 0, 0)),
            pl.BlockSpec((1, d), lambda bi, si: (0, 0)),
        ],
        out_specs=pl.BlockSpec((1, tm, d), row),
        out_shape=jax.ShapeDtypeStruct((b, s, d), F32),
        scratch_shapes=[pltpu.VMEM((HALO + tm, POOL_WIDTH), F32)],
        compiler_params=pltpu.CompilerParams(
            dimension_semantics=("arbitrary", "arbitrary"), vmem_limit_bytes=VMEM_LIMIT),
        name="out",
    )(x, u, pg, a, pool_w, pool_scale, w_out, final_g)


def kernel(x, norm_g, w_in, forget_bias, pool_w, pool_scale, w_out, final_g):
    wide = N_SEG * SEG
    w_main = w_in[:, :, :wide].astype(BF16)
    w_f = jnp.pad(w_in[:, :, wide:], ((0, 0), (0, 0), (0, LANES - ATTN_HEADS))).astype(BF16)
    fb = forget_bias.astype(F32)[:, :, None]
    g = norm_g.astype(F32)[:, None, :]
    pw = pool_w.astype(BF16)
    ps = pool_scale.astype(F32)[:, None, :]
    wo = w_out.astype(BF16)
    fg = final_g.astype(F32)[None, :]
    idx = jnp.arange(LANES)
    tri = (idx[:, None] <= idx[None, :]).astype(F32)
    ones = jnp.ones((LANES, LANES), F32)

    for layer in range(DEPTH):
        u, pg, q, k, v, ag, c = _proj(x, g, w_main, w_f, fb, tri, ones, layer)
        a = _attn(q, k, v, c, ag)
        x = _out(x, u, pg, a, pw, ps, wo, fg, layer, layer == DEPTH - 1)
    return x
```

```python
import functools

import jax
import jax.numpy as jnp
from jax import lax
from jax.experimental import pallas as pl
from jax.experimental.pallas import tpu as pltpu

F32 = jnp.float32
BF16 = jnp.bfloat16

D_MODEL = 1024
DEPTH = 4
POOL_WIDTH = 512
POOL_WINDOWS = (2, 4, 8, 16)
POOL_GROUP_DIM = 128
HEAD_DIM = 64
ATTN_WIDTH = 512
ATTN_HEADS = 8
SEG = 512
N_SEG = 6
RMS_EPS = 1e-6
NEG_INF = -1e30

LANES = 128
HALO = 16
VMEM_LIMIT = 56 * 1024 * 1024

ROW_TILE = 512
Q_TILE = 512
KV_TILE = 512
ROW_BLOCK = 256
LOG2E = 1.4426950408889634


def _silu(g):
    return g * (1.0 / (1.0 + jnp.exp(-g)))


def _log_sigmoid(z):
    return jnp.minimum(z, 0.0) - jnp.log1p(jnp.exp(-jnp.abs(z)))


def _proj_kernel(x_ref, g_ref, w_ref, wf_ref, fb_ref, tri_ref, ones_ref,
                 u_ref, pg_ref, q_ref, k_ref, v_ref, ag_ref, c_ref, carry_ref):
    @pl.when(pl.program_id(1) == 0)
    def _():
        carry_ref[...] = jnp.zeros_like(carry_ref)

    x = x_ref[0]
    ms = jnp.mean(x * x, axis=-1, keepdims=True)
    h = (x * lax.rsqrt(ms + RMS_EPS) * g_ref[...]).astype(BF16)

    def seg(i):
        return jnp.dot(h, w_ref[:, i * SEG:(i + 1) * SEG], preferred_element_type=F32)

    u_ref[0] = seg(0)
    pg_ref[0] = seg(1)
    q_ref[0] = (seg(2) * (HEAD_DIM ** -0.5 * LOG2E)).astype(BF16)
    k_ref[0] = seg(3).astype(BF16)
    v_ref[0] = seg(4).astype(BF16)
    ag_ref[0] = seg(5)

    z = jnp.dot(h, wf_ref[...], preferred_element_type=F32)
    lf = _log_sigmoid(z.T[:ATTN_HEADS] + fb_ref[...])
    tm = lf.shape[1]
    carry = carry_ref[...]
    for j in range(tm // LANES):
        blk = lf[:, j * LANES:(j + 1) * LANES]
        loc = jnp.dot(blk, tri_ref[...], preferred_element_type=F32, precision=lax.Precision.HIGHEST)
        c_ref[0, :, j * LANES:(j + 1) * LANES] = (loc + carry) * LOG2E
        carry = carry + jnp.dot(blk, ones_ref[...], preferred_element_type=F32,
                                precision=lax.Precision.HIGHEST)
    carry_ref[...] = carry


def _proj(x, g, w_main, w_f, fb, tri, ones, layer):
    b, s, d = x.shape
    tm = ROW_TILE
    grid = (b, s // tm)
    row = lambda bi, si: (bi, si, 0)
    wide_f32 = jax.ShapeDtypeStruct((b, s, SEG), F32)
    wide_bf16 = jax.ShapeDtypeStruct((b, s, SEG), BF16)
    out_shape = (wide_f32, wide_f32, wide_bf16, wide_bf16, wide_bf16, wide_f32,
                 jax.ShapeDtypeStruct((b, ATTN_HEADS, s), F32))
    wide_spec = pl.BlockSpec((1, tm, SEG), row)
    return pl.pallas_call(
        _proj_kernel,
        grid=grid,
        in_specs=[
            pl.BlockSpec((1, tm, d), row),
            pl.BlockSpec((None, 1, d), lambda bi, si: (layer, 0, 0)),
            pl.BlockSpec((None, d, N_SEG * SEG), lambda bi, si: (layer, 0, 0)),
            pl.BlockSpec((None, d, LANES), lambda bi, si: (layer, 0, 0)),
            pl.BlockSpec((None, ATTN_HEADS, 1), lambda bi, si: (layer, 0, 0)),
            pl.BlockSpec((LANES, LANES), lambda bi, si: (0, 0)),
            pl.BlockSpec((LANES, LANES), lambda bi, si: (0, 0)),
        ],
        out_specs=(wide_spec,) * 6 + (pl.BlockSpec((1, ATTN_HEADS, tm), lambda bi, si: (bi, 0, si)),),
        out_shape=out_shape,
        scratch_shapes=[pltpu.VMEM((ATTN_HEADS, LANES), F32)],
        compiler_params=pltpu.CompilerParams(
            dimension_semantics=("arbitrary", "arbitrary"), vmem_limit_bytes=VMEM_LIMIT),
        name="proj",
    )(x, g, w_main, w_f, fb, tri, ones)


def _attn_kernel(q_ref, k_ref, v_ref, c_ref, ag_ref, o_ref, m_sc, acc_sc):
    qi = pl.program_id(2)
    tq, tk, rb = Q_TILE, KV_TILE, ROW_BLOCK
    lane = lax.broadcasted_iota(jnp.int32, (1, LANES), 1)
    first = lane < HEAD_DIM
    q = q_ref[0]
    zero = jnp.zeros_like(q)
    q_heads = (jnp.where(first, q, zero), jnp.where(first, zero, q))

    m_sc[...] = jnp.full_like(m_sc, NEG_INF)
    acc_sc[...] = jnp.zeros_like(acc_sc)

    def step(j, diagonal):
        start = pl.multiple_of(j * tk, tk)
        ks = k_ref[0, pl.ds(start, tk), :]
        vs = v_ref[0, pl.ds(start, tk), :]
        one = jnp.ones_like(vs)
        v_heads = (jnp.where(first, vs, one), jnp.where(first, one, vs))
        bias = [c_ref[0, 0, hh:hh + 1, pl.ds(start, tk)] for hh in range(2)]

        def keys(r):
            return (r + 1) * rb if diagonal else tk

        def scores(hh, r):
            n = keys(r)
            s = lax.dot_general(q_heads[hh][r * rb:(r + 1) * rb], ks[:n], (((1,), (1,)), ((), ())),
                                preferred_element_type=F32)
            return s - bias[hh][:, :n]

        def finish(hh, r, s):
            n = keys(r)
            rows = slice(r * rb, (r + 1) * rb)
            if diagonal:
                row = r * rb + lax.broadcasted_iota(jnp.int32, (rb, n), 0)
                col = lax.broadcasted_iota(jnp.int32, (rb, n), 1)
                s = jnp.where(row >= col, s, NEG_INF)
            tiles = [s[:, c * LANES:(c + 1) * LANES] for c in range(n // LANES)]
            smax = functools.reduce(jnp.maximum, tiles)
            m_old = m_sc[hh, rows, :]
            m_new = jnp.maximum(m_old, jnp.max(smax, axis=-1, keepdims=True))
            alpha = jnp.exp2(m_old - m_new)
            p = jnp.concatenate([jnp.exp2(t - m_new).astype(BF16) for t in tiles], axis=1)
            pv = jnp.dot(p, v_heads[hh][:n], preferred_element_type=F32)
            acc_sc[hh, rows, :] = alpha * acc_sc[hh, rows, :] + pv
            m_sc[hh, rows, :] = m_new

        units = [(hh, r) for r in range(tq // rb) for hh in range(2)]
        pending = scores(*units[0])
        for i, unit in enumerate(units):
            nxt = scores(*units[i + 1]) if i + 1 < len(units) else None
            finish(*unit, pending)
            pending = nxt

    def body(j, carry):
        step(j, False)
        return carry

    lax.fori_loop(0, qi, body, 0)
    step(qi, True)

    acc0, acc1 = acc_sc[0], acc_sc[1]
    out = jnp.where(first, acc0 / pltpu.roll(acc0, HEAD_DIM, axis=1),
                    acc1 / pltpu.roll(acc1, HEAD_DIM, axis=1))
    o_ref[0] = (out * _silu(ag_ref[0])).astype(BF16)


def _attn(q, k, v, c, ag):
    b, s, _ = q.shape
    pairs = ATTN_WIDTH // LANES
    tq = Q_TILE
    grid = (b, pairs, s // tq)
    tile = pl.BlockSpec((1, tq, LANES), lambda bi, hp, qi: (bi, qi, hp))
    full = pl.BlockSpec((1, s, LANES), lambda bi, hp, qi: (bi, 0, hp))
    return pl.pallas_call(
        _attn_kernel,
        grid=grid,
        in_specs=[tile, full, full,
                  pl.BlockSpec((1, 1, 2, s), lambda bi, hp, qi: (bi, hp, 0, 0)),
                  tile],
        out_specs=tile,
        out_shape=jax.ShapeDtypeStruct((b, s, ATTN_WIDTH), BF16),
        scratch_shapes=[pltpu.VMEM((2, tq, LANES), F32), pltpu.VMEM((2, tq, LANES), F32)],
        compiler_params=pltpu.CompilerParams(
            dimension_semantics=("arbitrary", "arbitrary", "arbitrary"), vmem_limit_bytes=VMEM_LIMIT),
        name="attn",
    )(q, k, v, c.reshape(b, pairs, 2, s), ag)


def _out_kernel(x_ref, u_ref, pg_ref, a_ref, pw_ref, ps_ref, wo_ref, fg_ref, o_ref, ext_ref, *, final):
    si = pl.program_id(1)
    tm = u_ref.shape[1]

    @pl.when(si == 0)
    def _():
        ext_ref[0:HALO, :] = jnp.zeros((HALO, POOL_WIDTH), F32)

    u = u_ref[0]
    ext_ref[HALO:, :] = u
    pos = si * tm + lax.broadcasted_iota(jnp.int32, (tm, 1), 0)
    parts = []
    for gi, w in enumerate(POOL_WINDOWS):
        cols = slice(gi * POOL_GROUP_DIM, (gi + 1) * POOL_GROUP_DIM)
        win = ext_ref[:, cols]
        for st in range(gi + 1):
            win = win + pltpu.roll(win, 1 << st, axis=0)
        count = jnp.minimum(pos + 1, w).astype(F32)
        dlt = win[HALO:] / count - u[:, cols]
        parts.append(jnp.dot(dlt.astype(BF16), pw_ref[gi], preferred_element_type=F32))
    ext_ref[0:HALO, :] = u[tm - HALO:]
    y = jnp.concatenate(parts, axis=1) * ps_ref[...]
    pool_out = (y * _silu(pg_ref[0])).astype(BF16)
    acc = jnp.dot(pool_out, wo_ref[0:POOL_WIDTH, :], preferred_element_type=F32)
    acc = acc + jnp.dot(a_ref[0], wo_ref[POOL_WIDTH:, :], preferred_element_type=F32)
    xn = x_ref[0] + acc
    if final:
        ms = jnp.mean(xn * xn, axis=-1, keepdims=True)
        xn = xn * lax.rsqrt(ms + RMS_EPS) * fg_ref[...]
    o_ref[0] = xn


def _out(x, u, pg, a, pool_w, pool_scale, w_out, final_g, layer, final):
    b, s, d = x.shape
    tm = ROW_TILE
    row = lambda bi, si: (bi, si, 0)
    wide = pl.BlockSpec((1, tm, SEG), row)
    return pl.pallas_call(
        functools.partial(_out_kernel, final=final),
        grid=(b, s // tm),
        in_specs=[
            pl.BlockSpec((1, tm, d), row), wide, wide, wide,
            pl.BlockSpec((None, len(POOL_WINDOWS), POOL_GROUP_DIM, POOL_GROUP_DIM),
                         lambda bi, si: (layer, 0, 0, 0)),
            pl.BlockSpec((None, 1, POOL_WIDTH), lambda bi, si: (layer, 0, 0)),
            pl.BlockSpec((None, d, d), lambda bi, si: (layer, 0, 0)),
            pl.BlockSpec((1, d), lambda bi, si: (0, 0)),
        ],
        out_specs=pl.BlockSpec((1, tm, d), row),
        out_shape=jax.ShapeDtypeStruct((b, s, d), F32),
        scratch_shapes=[pltpu.VMEM((HALO + tm, POOL_WIDTH), F32)],
        compiler_params=pltpu.CompilerParams(
            dimension_semantics=("arbitrary", "arbitrary"), vmem_limit_bytes=VMEM_LIMIT),
        name="out",
    )(x, u, pg, a, pool_w, pool_scale, w_out, final_g)


def kernel(x, norm_g, w_in, forget_bias, pool_w, pool_scale, w_out, final_g):
    wide = N_SEG * SEG
    w_main = w_in[:, :, :wide].astype(BF16)
    w_f = jnp.pad(w_in[:, :, wide:], ((0, 0), (0, 0), (0, LANES - ATTN_HEADS))).astype(BF16)
    fb = forget_bias.astype(F32)[:, :, None]
    g = norm_g.astype(F32)[:, None, :]
    pw = pool_w.astype(BF16)
    ps = pool_scale.astype(F32)[:, None, :]
    wo = w_out.astype(BF16)
    fg = final_g.astype(F32)[None, :]
    idx = jnp.arange(LANES)
    tri = (idx[:, None] <= idx[None, :]).astype(F32)
    ones = jnp.ones((LANES, LANES), F32)

    for layer in range(DEPTH):
        u, pg, q, k, v, ag, c = _proj(x, g, w_main, w_f, fb, tri, ones, layer)
        a = _attn(q, k, v, c, ag)
        x = _out(x, u, pg, a, pw, ps, wo, fg, layer, layer == DEPTH - 1)
    return x
```

```python
import functools

import jax
import jax.numpy as jnp
from jax import lax
from jax.experimental import pallas as pl
from jax.experimental.pallas import tpu as pltpu

F32 = jnp.float32
BF16 = jnp.bfloat16

D_MODEL = 1024
DEPTH = 4
POOL_WIDTH = 512
POOL_WINDOWS = (2, 4, 8, 16)
POOL_GROUP_DIM = 128
HEAD_DIM = 64
ATTN_WIDTH = 512
ATTN_HEADS = 8
SEG = 512
N_SEG = 6
RMS_EPS = 1e-6
NEG_INF = -1e30

LANES = 128
HALO = 16
VMEM_LIMIT = 56 * 1024 * 1024

ROW_TILE = 512
Q_TILE = 1024
KV_TILE = 512
ROW_BLOCK = 256
LOG2E = 1.4426950408889634


def _silu(g):
    return g * (1.0 / (1.0 + jnp.exp(-g)))


def _log_sigmoid(z):
    return jnp.minimum(z, 0.0) - jnp.log1p(jnp.exp(-jnp.abs(z)))


def _proj_kernel(x_ref, g_ref, w_ref, wf_ref, fb_ref, tri_ref, ones_ref,
                 u_ref, pg_ref, q_ref, k_ref, v_ref, ag_ref, c_ref, carry_ref):
    @pl.when(pl.program_id(1) == 0)
    def _():
        carry_ref[...] = jnp.zeros_like(carry_ref)

    x = x_ref[0]
    ms = jnp.mean(x * x, axis=-1, keepdims=True)
    h = (x * lax.rsqrt(ms + RMS_EPS) * g_ref[...]).astype(BF16)

    def seg(i):
        return jnp.dot(h, w_ref[:, i * SEG:(i + 1) * SEG], preferred_element_type=F32)

    u_ref[0] = seg(0)
    pg_ref[0] = seg(1)
    q_ref[0] = (seg(2) * (HEAD_DIM ** -0.5 * LOG2E)).astype(BF16)
    k_ref[0] = seg(3).astype(BF16)
    v_ref[0] = seg(4).astype(BF16)
    ag_ref[0] = seg(5)

    z = jnp.dot(h, wf_ref[...], preferred_element_type=F32)
    lf = _log_sigmoid(z.T[:ATTN_HEADS] + fb_ref[...])
    tm = lf.shape[1]
    carry = carry_ref[...]
    for j in range(tm // LANES):
        blk = lf[:, j * LANES:(j + 1) * LANES]
        loc = jnp.dot(blk, tri_ref[...], preferred_element_type=F32, precision=lax.Precision.HIGHEST)
        c_ref[0, :, j * LANES:(j + 1) * LANES] = (loc + carry) * LOG2E
        carry = carry + jnp.dot(blk, ones_ref[...], preferred_element_type=F32,
                                precision=lax.Precision.HIGHEST)
    carry_ref[...] = carry


def _proj(x, g, w_main, w_f, fb, tri, ones, layer):
    b, s, d = x.shape
    tm = ROW_TILE
    grid = (b, s // tm)
    row = lambda bi, si: (bi, si, 0)
    wide_f32 = jax.ShapeDtypeStruct((b, s, SEG), F32)
    wide_bf16 = jax.ShapeDtypeStruct((b, s, SEG), BF16)
    out_shape = (wide_f32, wide_f32, wide_bf16, wide_bf16, wide_bf16, wide_f32,
                 jax.ShapeDtypeStruct((b, ATTN_HEADS, s), F32))
    wide_spec = pl.BlockSpec((1, tm, SEG), row)
    return pl.pallas_call(
        _proj_kernel,
        grid=grid,
        in_specs=[
            pl.BlockSpec((1, tm, d), row),
            pl.BlockSpec((None, 1, d), lambda bi, si: (layer, 0, 0)),
            pl.BlockSpec((None, d, N_SEG * SEG), lambda bi, si: (layer, 0, 0)),
            pl.BlockSpec((None, d, LANES), lambda bi, si: (layer, 0, 0)),
            pl.BlockSpec((None, ATTN_HEADS, 1), lambda bi, si: (layer, 0, 0)),
            pl.BlockSpec((LANES, LANES), lambda bi, si: (0, 0)),
            pl.BlockSpec((LANES, LANES), lambda bi, si: (0, 0)),
        ],
        out_specs=(wide_spec,) * 6 + (pl.BlockSpec((1, ATTN_HEADS, tm), lambda bi, si: (bi, 0, si)),),
        out_shape=out_shape,
        scratch_shapes=[pltpu.VMEM((ATTN_HEADS, LANES), F32)],
        compiler_params=pltpu.CompilerParams(
            dimension_semantics=("arbitrary", "arbitrary"), vmem_limit_bytes=VMEM_LIMIT),
        name="proj",
    )(x, g, w_main, w_f, fb, tri, ones)


def _attn_kernel(q_ref, k_ref, v_ref, c_ref, ag_ref, o_ref, m_sc, l_sc, acc_sc):
    qi = pl.program_id(2)
    tq, tk, rb = Q_TILE, KV_TILE, ROW_BLOCK
    lane = lax.broadcasted_iota(jnp.int32, (1, LANES), 1)
    first = lane < HEAD_DIM
    q = q_ref[0]
    zero = jnp.zeros_like(q)
    q0, q1 = jnp.where(first, q, zero), jnp.where(first, zero, q)
    q_units = [jnp.concatenate([q0[r * rb:(r + 1) * rb], q1[r * rb:(r + 1) * rb]], axis=0)
               for r in range(tq // rb)]

    m_sc[...] = jnp.full_like(m_sc, NEG_INF)
    l_sc[...] = jnp.zeros_like(l_sc)
    acc_sc[...] = jnp.zeros_like(acc_sc)

    def step(j, diag):
        start = pl.multiple_of(j * tk, tk)
        ks = k_ref[0, pl.ds(start, tk), :]
        vs = v_ref[0, pl.ds(start, tk), :]
        bias = [c_ref[0, 0, hh:hh + 1, pl.ds(start, tk)] for hh in range(2)]

        def visible(r):
            if diag is None:
                return tk
            return max(0, min(tk, (r + 1) * rb - diag * tk))

        def scores(r):
            n = visible(r)
            return lax.dot_general(q_units[r], ks[:n], (((1,), (1,)), ((), ())),
                                   preferred_element_type=F32)

        def finish(r, s2):
            n = visible(r)
            rows = slice(r * rb, (r + 1) * rb)
            masked = diag is not None and r * rb < diag * tk + n
            ps = []
            for hh in range(2):
                s = s2[hh * rb:(hh + 1) * rb] - bias[hh][:, :n]
                if masked:
                    row = r * rb + lax.broadcasted_iota(jnp.int32, (rb, n), 0)
                    col = diag * tk + lax.broadcasted_iota(jnp.int32, (rb, n), 1)
                    s = jnp.where(row >= col, s, NEG_INF)
                tiles = [s[:, c * LANES:(c + 1) * LANES] for c in range(n // LANES)]
                smax = functools.reduce(jnp.maximum, tiles)
                m_old = m_sc[hh, rows, :]
                m_new = jnp.maximum(m_old, jnp.max(smax, axis=-1, keepdims=True))
                alpha = jnp.exp2(m_old - m_new)
                es = [jnp.exp2(x - m_new) for x in tiles]
                esum = functools.reduce(jnp.add, es)
                l_sc[hh, rows, :] = alpha * l_sc[hh, rows, :] + jnp.sum(esum, axis=-1, keepdims=True)
                acc_sc[hh, rows, :] = alpha * acc_sc[hh, rows, :]
                m_sc[hh, rows, :] = m_new
                ps.append(jnp.concatenate([x.astype(BF16) for x in es], axis=1))
            pv = jnp.dot(jnp.concatenate(ps, axis=0), vs[:n], preferred_element_type=F32)
            for hh in range(2):
                acc_sc[hh, rows, :] += pv[hh * rb:(hh + 1) * rb]

        units = [r for r in range(tq // rb) if visible(r) > 0]
        pending = scores(units[0])
        for i, r in enumerate(units):
            nxt = scores(units[i + 1]) if i + 1 < len(units) else None
            finish(r, pending)
            pending = nxt

    def body(j, carry):
        step(j, None)
        return carry

    per = tq // tk
    lax.fori_loop(0, qi * per, body, 0)
    for d in range(per):
        step(qi * per + d, d)

    out = jnp.where(first, acc_sc[0] / l_sc[0], acc_sc[1] / l_sc[1])
    o_ref[0] = (out * _silu(ag_ref[0])).astype(BF16)


def _attn(q, k, v, c, ag):
    b, s, _ = q.shape
    pairs = ATTN_WIDTH // LANES
    tq = Q_TILE
    grid = (b, pairs, s // tq)
    tile = pl.BlockSpec((1, tq, LANES), lambda bi, hp, qi: (bi, qi, hp))
    full = pl.BlockSpec((1, s, LANES), lambda bi, hp, qi: (bi, 0, hp))
    return pl.pallas_call(
        _attn_kernel,
        grid=grid,
        in_specs=[tile, full, full,
                  pl.BlockSpec((1, 1, 2, s), lambda bi, hp, qi: (bi, hp, 0, 0)),
                  tile],
        out_specs=tile,
        out_shape=jax.ShapeDtypeStruct((b, s, ATTN_WIDTH), BF16),
        scratch_shapes=[pltpu.VMEM((2, tq, LANES), F32)] * 3,
        compiler_params=pltpu.CompilerParams(
            dimension_semantics=("arbitrary", "arbitrary", "arbitrary"), vmem_limit_bytes=VMEM_LIMIT),
        name="attn",
    )(q, k, v, c.reshape(b, pairs, 2, s), ag)


def _out_kernel(x_ref, u_ref, pg_ref, a_ref, pw_ref, ps_ref, wo_ref, fg_ref, o_ref, ext_ref, *, final):
    si = pl.program_id(1)
    tm = u_ref.shape[1]

    @pl.when(si == 0)
    def _():
        ext_ref[0:HALO, :] = jnp.zeros((HALO, POOL_WIDTH), F32)

    u = u_ref[0]
    ext_ref[HALO:, :] = u
    pos = si * tm + lax.broadcasted_iota(jnp.int32, (tm, 1), 0)
    parts = []
    for gi, w in enumerate(POOL_WINDOWS):
        cols = slice(gi * POOL_GROUP_DIM, (gi + 1) * POOL_GROUP_DIM)
        win = ext_ref[:, cols]
        for st in range(gi + 1):
            win = win + pltpu.roll(win, 1 << st, axis=0)
        count = jnp.minimum(pos + 1, w).astype(F32)
        dlt = win[HALO:] / count - u[:, cols]
        parts.append(jnp.dot(dlt.astype(BF16), pw_ref[gi], preferred_element_type=F32))
    ext_ref[0:HALO, :] = u[tm - HALO:]
    y = jnp.concatenate(parts, axis=1) * ps_ref[...]
    pool_out = (y * _silu(pg_ref[0])).astype(BF16)
    acc = jnp.dot(pool_out, wo_ref[0:POOL_WIDTH, :], preferred_element_type=F32)
    acc = acc + jnp.dot(a_ref[0], wo_ref[POOL_WIDTH:, :], preferred_element_type=F32)
    xn = x_ref[0] + acc
    if final:
        ms = jnp.mean(xn * xn, axis=-1, keepdims=True)
        xn = xn * lax.rsqrt(ms + RMS_EPS) * fg_ref[...]
    o_ref[0] = xn


def _out(x, u, pg, a, pool_w, pool_scale, w_out, final_g, layer, final):
    b, s, d = x.shape
    tm = ROW_TILE
    row = lambda bi, si: (bi, si, 0)
    wide = pl.BlockSpec((1, tm, SEG), row)
    return pl.pallas_call(
        functools.partial(_out_kernel, final=final),
        grid=(b, s // tm),
        in_specs=[
            pl.BlockSpec((1, tm, d), row), wide, wide, wide,
            pl.BlockSpec((None, len(POOL_WINDOWS), POOL_GROUP_DIM, POOL_GROUP_DIM),
                         lambda bi, si: (layer, 0, 0, 0)),
            pl.BlockSpec((None, 1, POOL_WIDTH), lambda bi, si: (layer, 0, 0)),
            pl.BlockSpec((None, d, d), lambda bi, si: (layer, 0, 0)),
            pl.BlockSpec((1, d), lambda bi, si: (0, 0)),
        ],
        out_specs=pl.BlockSpec((1, tm, d), row),
        out_shape=jax.ShapeDtypeStruct((b, s, d), F32),
        scratch_shapes=[pltpu.VMEM((HALO + tm, POOL_WIDTH), F32)],
        compiler_params=pltpu.CompilerParams(
            dimension_semantics=("arbitrary", "arbitrary"), vmem_limit_bytes=VMEM_LIMIT),
        name="out",
    )(x, u, pg, a, pool_w, pool_scale, w_out, final_g)


def kernel(x, norm_g, w_in, forget_bias, pool_w, pool_scale, w_out, final_g):
    wide = N_SEG * SEG
    w_main = w_in[:, :, :wide].astype(BF16)
    w_f = jnp.pad(w_in[:, :, wide:], ((0, 0), (0, 0), (0, LANES - ATTN_HEADS))).astype(BF16)
    fb = forget_bias.astype(F32)[:, :, None]
    g = norm_g.astype(F32)[:, None, :]
    pw = pool_w.astype(BF16)
    ps = pool_scale.astype(F32)[:, None, :]
    wo = w_out.astype(BF16)
    fg = final_g.astype(F32)[None, :]
    idx = jnp.arange(LANES)
    tri = (idx[:, None] <= idx[None, :]).astype(F32)
    ones = jnp.ones((LANES, LANES), F32)

    for layer in range(DEPTH):
        u, pg, q, k, v, ag, c = _proj(x, g, w_main, w_f, fb, tri, ones, layer)
        a = _attn(q, k, v, c, ag)
        x = _out(x, u, pg, a, pw, ps, wo, fg, layer, layer == DEPTH - 1)
    return x
```

```python
import functools

import jax
import jax.numpy as jnp
from jax import lax
from jax.experimental import pallas as pl
from jax.experimental.pallas import tpu as pltpu

F32 = jnp.float32
BF16 = jnp.bfloat16

D_MODEL = 1024
DEPTH = 4
POOL_WIDTH = 512
POOL_WINDOWS = (2, 4, 8, 16)
POOL_GROUP_DIM = 128
HEAD_DIM = 64
ATTN_WIDTH = 512
ATTN_HEADS = 8
SEG = 512
N_SEG = 6
RMS_EPS = 1e-6
NEG_INF = -1e30

LANES = 128
HALO = 16
VMEM_LIMIT = 56 * 1024 * 1024

ROW_TILE = 512
Q_TILE = 1024
KV_TILE = 512
ROW_BLOCK = 256
LOOKAHEAD = 1
LOG2E = 1.4426950408889634


def _silu(g):
    return g * (1.0 / (1.0 + jnp.exp(-g)))


def _log_sigmoid(z):
    return jnp.minimum(z, 0.0) - jnp.log1p(jnp.exp(-jnp.abs(z)))


def _proj_kernel(x_ref, g_ref, w_ref, wf_ref, fb_ref, prefix_ref,
                 u_ref, pg_ref, q_ref, k_ref, v_ref, ag_ref, c_ref, carry_ref):
    @pl.when(pl.program_id(1) == 0)
    def _():
        carry_ref[...] = jnp.zeros_like(carry_ref)

    x = x_ref[0]
    ms = jnp.mean(x * x, axis=-1, keepdims=True)
    h = (x * lax.rsqrt(ms + RMS_EPS) * g_ref[...]).astype(BF16)

    def seg(i):
        return jnp.dot(h, w_ref[:, i * SEG:(i + 1) * SEG], preferred_element_type=F32)

    z = jnp.dot(h, wf_ref[...], preferred_element_type=F32)
    u_ref[0] = seg(0)
    pg_ref[0] = seg(1)
    q_ref[0] = (seg(2) * (HEAD_DIM ** -0.5 * LOG2E)).astype(BF16)
    k_ref[0] = seg(3).astype(BF16)

    lf = _log_sigmoid(z.T[:ATTN_HEADS] + fb_ref[...])
    tm = lf.shape[1]
    hi = lf.astype(BF16).astype(F32)
    rest = lf - hi
    mid = rest.astype(BF16).astype(F32)
    parts = jnp.concatenate([hi, mid, rest - mid, jnp.zeros_like(lf)], axis=0).astype(BF16)
    carry = carry_ref[...]
    for j in range(tm // LANES):
        res = jnp.dot(parts[:, j * LANES:(j + 1) * LANES], prefix_ref[...], preferred_element_type=F32)
        res = res[0:8] + res[8:16] + res[16:24]
        c_ref[0, :, j * LANES:(j + 1) * LANES] = (res[:, :LANES] + carry) * LOG2E
        carry = carry + res[:, LANES:]
    carry_ref[...] = carry

    v_ref[0] = seg(4).astype(BF16)
    ag_ref[0] = seg(5)


def _proj(x, g, w_main, w_f, fb, prefix, layer):
    b, s, d = x.shape
    tm = ROW_TILE
    grid = (b, s // tm)
    row = lambda bi, si: (bi, si, 0)
    wide_f32 = jax.ShapeDtypeStruct((b, s, SEG), F32)
    wide_bf16 = jax.ShapeDtypeStruct((b, s, SEG), BF16)
    out_shape = (wide_f32, wide_f32, wide_bf16, wide_bf16, wide_bf16, wide_f32,
                 jax.ShapeDtypeStruct((b, ATTN_HEADS, s), F32))
    wide_spec = pl.BlockSpec((1, tm, SEG), row)
    return pl.pallas_call(
        _proj_kernel,
        grid=grid,
        in_specs=[
            pl.BlockSpec((1, tm, d), row),
            pl.BlockSpec((None, 1, d), lambda bi, si: (layer, 0, 0)),
            pl.BlockSpec((None, d, N_SEG * SEG), lambda bi, si: (layer, 0, 0)),
            pl.BlockSpec((None, d, LANES), lambda bi, si: (layer, 0, 0)),
            pl.BlockSpec((None, ATTN_HEADS, 1), lambda bi, si: (layer, 0, 0)),
            pl.BlockSpec((LANES, 2 * LANES), lambda bi, si: (0, 0)),
        ],
        out_specs=(wide_spec,) * 6 + (pl.BlockSpec((1, ATTN_HEADS, tm), lambda bi, si: (bi, 0, si)),),
        out_shape=out_shape,
        scratch_shapes=[pltpu.VMEM((ATTN_HEADS, LANES), F32)],
        compiler_params=pltpu.CompilerParams(
            dimension_semantics=("arbitrary", "arbitrary"), vmem_limit_bytes=VMEM_LIMIT),
        name="proj",
    )(x, g, w_main, w_f, fb, prefix)


def _attn_kernel(q_ref, k_ref, v_ref, c_ref, ag_ref, o_ref, m_sc, l_sc, acc_sc):
    qi = pl.program_id(2)
    tq, tk, rb = Q_TILE, KV_TILE, ROW_BLOCK
    lane = lax.broadcasted_iota(jnp.int32, (1, LANES), 1)
    first = lane < HEAD_DIM
    q = q_ref[0]
    zero = jnp.zeros_like(q)
    q0, q1 = jnp.where(first, q, zero), jnp.where(first, zero, q)
    q_units = [jnp.concatenate([q0[r * rb:(r + 1) * rb], q1[r * rb:(r + 1) * rb]], axis=0)
               for r in range(tq // rb)]

    m_sc[...] = jnp.full_like(m_sc, NEG_INF)
    l_sc[...] = jnp.zeros_like(l_sc)
    acc_sc[...] = jnp.zeros_like(acc_sc)
    per = tq // tk
    blocks = tq // rb

    def load_tile(j):
        start = pl.multiple_of(j * tk, tk)
        vs = v_ref[0, pl.ds(start, tk), :]
        vs1 = jnp.concatenate([vs, jnp.ones_like(vs)], axis=1)
        bias = [c_ref[0, 0, hh:hh + 1, pl.ds(start, tk)] for hh in range(2)]
        return k_ref[0, pl.ds(start, tk), :], vs1, bias

    def scores(unit):
        r, n, inputs, _ = unit
        return lax.dot_general(q_units[r], inputs[0][:n], (((1,), (1,)), ((), ())),
                               preferred_element_type=F32)

    def finish(unit, s2):
        r, n, (_, vs1, bias), col0 = unit
        rows = slice(r * rb, (r + 1) * rb)
        ps, alphas = [], []
        for hh in range(2):
            s = s2[hh * rb:(hh + 1) * rb] - bias[hh][:, :n]
            if col0 is not None:
                row = r * rb + lax.broadcasted_iota(jnp.int32, (rb, n), 0)
                col = col0 + lax.broadcasted_iota(jnp.int32, (rb, n), 1)
                s = jnp.where(row >= col, s, NEG_INF)
            tiles = [s[:, c * LANES:(c + 1) * LANES] for c in range(n // LANES)]
            smax = functools.reduce(jnp.maximum, tiles)
            m_old = m_sc[hh, rows, :]
            m_new = jnp.maximum(m_old, jnp.max(smax, axis=-1, keepdims=True))
            alphas.append(jnp.exp2(m_old - m_new))
            m_sc[hh, rows, :] = m_new
            ps.append(jnp.concatenate([jnp.exp2((x - m_new).astype(BF16)) for x in tiles], axis=1))
        pv = jnp.dot(jnp.concatenate(ps, axis=0), vs1[:n], preferred_element_type=F32)
        for hh in range(2):
            part = pv[hh * rb:(hh + 1) * rb]
            acc_sc[hh, rows, :] = alphas[hh] * acc_sc[hh, rows, :] + part[:, :LANES]
            l_sc[hh, rows, :] = alphas[hh] * l_sc[hh, rows, :] + part[:, LANES:]

    def run(units):
        pending = [scores(u) for u in units[:LOOKAHEAD]]
        for i, unit in enumerate(units):
            if i + LOOKAHEAD < len(units):
                pending.append(scores(units[i + LOOKAHEAD]))
            finish(unit, pending.pop(0))

    def body(i, carry):
        units = []
        for d in range(per):
            inputs = load_tile(i * per + d)
            units += [(r, tk, inputs, None) for r in range(blocks)]
        run(units)
        return carry

    lax.fori_loop(0, qi, body, 0)

    units = []
    for d in range(per):
        inputs = load_tile(qi * per + d)
        for r in range(blocks):
            n = max(0, min(tk, (r + 1) * rb - d * tk))
            if n > 0:
                causal = r * rb < d * tk + n
                units.append((r, n, inputs, d * tk if causal else None))
    run(units)

    out = jnp.where(first, acc_sc[0] / l_sc[0], acc_sc[1] / l_sc[1])
    o_ref[0] = (out * _silu(ag_ref[0])).astype(BF16)


def _attn(q, k, v, c, ag):
    b, s, _ = q.shape
    pairs = ATTN_WIDTH // LANES
    tq = Q_TILE
    grid = (b, pairs, s // tq)
    tile = pl.BlockSpec((1, tq, LANES), lambda bi, hp, qi: (bi, qi, hp))
    full = pl.BlockSpec((1, s, LANES), lambda bi, hp, qi: (bi, 0, hp))
    return pl.pallas_call(
        _attn_kernel,
        grid=grid,
        in_specs=[tile, full, full,
                  pl.BlockSpec((1, 1, 2, s), lambda bi, hp, qi: (bi, hp, 0, 0)),
                  tile],
        out_specs=tile,
        out_shape=jax.ShapeDtypeStruct((b, s, ATTN_WIDTH), BF16),
        scratch_shapes=[pltpu.VMEM((2, tq, LANES), F32)] * 3,
        compiler_params=pltpu.CompilerParams(
            dimension_semantics=("arbitrary", "arbitrary", "arbitrary"), vmem_limit_bytes=VMEM_LIMIT),
        name="attn",
    )(q, k, v, c.reshape(b, pairs, 2, s), ag)


def _out_kernel(x_ref, u_ref, pg_ref, a_ref, pw_ref, ps_ref, wo_ref, fg_ref, o_ref, ext_ref, *, final):
    si = pl.program_id(1)
    tm = u_ref.shape[1]

    @pl.when(si == 0)
    def _():
        ext_ref[0:HALO, :] = jnp.zeros((HALO, POOL_WIDTH), F32)

    u = u_ref[0]
    ext_ref[HALO:, :] = u
    pos = si * tm + lax.broadcasted_iota(jnp.int32, (tm, 1), 0)
    parts = []
    for gi, w in enumerate(POOL_WINDOWS):
        cols = slice(gi * POOL_GROUP_DIM, (gi + 1) * POOL_GROUP_DIM)
        win = ext_ref[:, cols]
        for st in range(gi + 1):
            win = win + pltpu.roll(win, 1 << st, axis=0)
        count = jnp.minimum(pos + 1, w).astype(F32)
        dlt = win[HALO:] / count - u[:, cols]
        parts.append(jnp.dot(dlt.astype(BF16), pw_ref[gi], preferred_element_type=F32))
    ext_ref[0:HALO, :] = u[tm - HALO:]
    y = jnp.concatenate(parts, axis=1) * ps_ref[...]
    pool_out = (y * _silu(pg_ref[0])).astype(BF16)
    acc = jnp.dot(pool_out, wo_ref[0:POOL_WIDTH, :], preferred_element_type=F32)
    acc = acc + jnp.dot(a_ref[0], wo_ref[POOL_WIDTH:, :], preferred_element_type=F32)
    xn = x_ref[0] + acc
    if final:
        ms = jnp.mean(xn * xn, axis=-1, keepdims=True)
        xn = xn * lax.rsqrt(ms + RMS_EPS) * fg_ref[...]
    o_ref[0] = xn


def _out(x, u, pg, a, pool_w, pool_scale, w_out, final_g, layer, final):
    b, s, d = x.shape
    tm = ROW_TILE
    row = lambda bi, si: (bi, si, 0)
    wide = pl.BlockSpec((1, tm, SEG), row)
    return pl.pallas_call(
        functools.partial(_out_kernel, final=final),
        grid=(b, s // tm),
        in_specs=[
            pl.BlockSpec((1, tm, d), row), wide, wide, wide,
            pl.BlockSpec((None, len(POOL_WINDOWS), POOL_GROUP_DIM, POOL_GROUP_DIM),
                         lambda bi, si: (layer, 0, 0, 0)),
            pl.BlockSpec((None, 1, POOL_WIDTH), lambda bi, si: (layer, 0, 0)),
            pl.BlockSpec((None, d, d), lambda bi, si: (layer, 0, 0)),
            pl.BlockSpec((1, d), lambda bi, si: (0, 0)),
        ],
        out_specs=pl.BlockSpec((1, tm, d), row),
        out_shape=jax.ShapeDtypeStruct((b, s, d), F32),
        scratch_shapes=[pltpu.VMEM((HALO + tm, POOL_WIDTH), F32)],
        compiler_params=pltpu.CompilerParams(
            dimension_semantics=("arbitrary", "arbitrary"), vmem_limit_bytes=VMEM_LIMIT),
        name="out",
    )(x, u, pg, a, pool_w, pool_scale, w_out, final_g)


def kernel(x, norm_g, w_in, forget_bias, pool_w, pool_scale, w_out, final_g):
    wide = N_SEG * SEG
    w_main = w_in[:, :, :wide].astype(BF16)
    w_f = jnp.pad(w_in[:, :, wide:], ((0, 0), (0, 0), (0, LANES - ATTN_HEADS))).astype(BF16)
    fb = forget_bias.astype(F32)[:, :, None]
    g = norm_g.astype(F32)[:, None, :]
    pw = pool_w.astype(BF16)
    ps = pool_scale.astype(F32)[:, None, :]
    wo = w_out.astype(BF16)
    fg = final_g.astype(F32)[None, :]
    idx = jnp.arange(LANES)
    prefix = jnp.concatenate([(idx[:, None] <= idx[None, :]).astype(BF16),
                              jnp.ones((LANES, LANES), BF16)], axis=1)

    for layer in range(DEPTH):
        u, pg, q, k, v, ag, c = _proj(x, g, w_main, w_f, fb, prefix, layer)
        a = _attn(q, k, v, c, ag)
        x = _out(x, u, pg, a, pw, ps, wo, fg, layer, layer == DEPTH - 1)
    return x
```

```python
import functools

import jax
import jax.numpy as jnp
from jax import lax
from jax.experimental import pallas as pl
from jax.experimental.pallas import tpu as pltpu

F32 = jnp.float32
BF16 = jnp.bfloat16

D_MODEL = 1024
DEPTH = 4
POOL_WIDTH = 512
POOL_WINDOWS = (2, 4, 8, 16)
POOL_GROUP_DIM = 128
HEAD_DIM = 64
ATTN_WIDTH = 512
ATTN_HEADS = 8
SEG = 512
N_SEG = 6
RMS_EPS = 1e-6
NEG_INF = -1e30

LANES = 128
HALO = 16
VMEM_LIMIT = 56 * 1024 * 1024

ROW_TILE = 512
Q_TILE = 1024
KV_TILE = 512
ROW_BLOCK = 256
LOOKAHEAD = 1
LOG2E = 1.4426950408889634


def _silu(g):
    return g * (1.0 / (1.0 + jnp.exp(-g)))


def _log_sigmoid(z):
    return jnp.minimum(z, 0.0) - jnp.log1p(jnp.exp(-jnp.abs(z)))


def _project(x, g_ref, w_ref, wf_ref, fb_ref, prefix_ref,
             u_ref, pg_ref, q_ref, k_ref, v_ref, ag_ref, c_ref, carry_ref):
    ms = jnp.mean(x * x, axis=-1, keepdims=True)
    h = (x * lax.rsqrt(ms + RMS_EPS) * g_ref[...]).astype(BF16)

    def seg(i):
        return jnp.dot(h, w_ref[:, i * SEG:(i + 1) * SEG], preferred_element_type=F32)

    z = jnp.dot(h, wf_ref[...], preferred_element_type=F32)
    u_ref[0] = seg(0)
    pg_ref[0] = seg(1)
    q_ref[0] = (seg(2) * (HEAD_DIM ** -0.5 * LOG2E)).astype(BF16)
    k_ref[0] = seg(3).astype(BF16)

    lf = _log_sigmoid(z.T[:ATTN_HEADS] + fb_ref[...])
    tm = lf.shape[1]
    hi = lf.astype(BF16).astype(F32)
    rest = lf - hi
    mid = rest.astype(BF16).astype(F32)
    parts = jnp.concatenate([hi, mid, rest - mid, jnp.zeros_like(lf)], axis=0).astype(BF16)
    carry = carry_ref[...]
    for j in range(tm // LANES):
        res = jnp.dot(parts[:, j * LANES:(j + 1) * LANES], prefix_ref[...], preferred_element_type=F32)
        res = res[0:8] + res[8:16] + res[16:24]
        c_ref[0, :, j * LANES:(j + 1) * LANES] = (res[:, :LANES] + carry) * LOG2E
        carry = carry + res[:, LANES:]
    carry_ref[...] = carry

    v_ref[0] = seg(4).astype(BF16)
    ag_ref[0] = seg(5)


def _mix(x_ref, u_ref, pg_ref, a_ref, pw_ref, ps_ref, wo_ref, ext_ref):
    si = pl.program_id(1)
    tm = u_ref.shape[1]

    @pl.when(si == 0)
    def _():
        ext_ref[0:HALO, :] = jnp.zeros((HALO, POOL_WIDTH), F32)

    acc = jnp.dot(a_ref[0], wo_ref[POOL_WIDTH:, :], preferred_element_type=F32)
    u = u_ref[0]
    ext_ref[HALO:, :] = u
    pos = si * tm + lax.broadcasted_iota(jnp.int32, (tm, 1), 0)
    parts = []
    for gi, w in enumerate(POOL_WINDOWS):
        cols = slice(gi * POOL_GROUP_DIM, (gi + 1) * POOL_GROUP_DIM)
        win = ext_ref[:, cols]
        for st in range(gi + 1):
            win = win + pltpu.roll(win, 1 << st, axis=0)
        count = jnp.minimum(pos + 1, w).astype(F32)
        dlt = win[HALO:] / count - u[:, cols]
        parts.append(jnp.dot(dlt.astype(BF16), pw_ref[gi], preferred_element_type=F32))
    ext_ref[0:HALO, :] = u[tm - HALO:]
    y = jnp.concatenate(parts, axis=1) * ps_ref[...]
    pool_out = (y * _silu(pg_ref[0])).astype(BF16)
    acc = acc + jnp.dot(pool_out, wo_ref[0:POOL_WIDTH, :], preferred_element_type=F32)
    return x_ref[0] + acc


def _zero_carry_at_sequence_start(carry_ref):
    @pl.when(pl.program_id(1) == 0)
    def _():
        carry_ref[...] = jnp.zeros_like(carry_ref)


def _first_kernel(x_ref, *refs):
    _zero_carry_at_sequence_start(refs[-1])
    _project(x_ref[0], *refs)


def _middle_kernel(x_ref, u_ref, pg_ref, a_ref, pw_ref, ps_ref, wo_ref, *refs):
    proj_in, (xo_ref, *proj_out), (ext_ref, carry_ref) = refs[:5], refs[5:13], refs[13:]
    _zero_carry_at_sequence_start(carry_ref)
    xn = _mix(x_ref, u_ref, pg_ref, a_ref, pw_ref, ps_ref, wo_ref, ext_ref)
    xo_ref[0] = xn
    _project(xn, *proj_in, *proj_out, carry_ref)


def _last_kernel(x_ref, u_ref, pg_ref, a_ref, pw_ref, ps_ref, wo_ref, fg_ref, o_ref, ext_ref):
    xn = _mix(x_ref, u_ref, pg_ref, a_ref, pw_ref, ps_ref, wo_ref, ext_ref)
    ms = jnp.mean(xn * xn, axis=-1, keepdims=True)
    o_ref[0] = xn * lax.rsqrt(ms + RMS_EPS) * fg_ref[...]


def _row(bi, si):
    return (bi, si, 0)


def _proj_specs(b, s, d, layer):
    tm = ROW_TILE
    pick = lambda bi, si: (layer, 0, 0)
    wide = N_SEG * SEG
    in_specs = [
        pl.BlockSpec((None, 1, d), pick),
        pl.BlockSpec((None, d, wide), pick),
        pl.BlockSpec((None, d, LANES), pick),
        pl.BlockSpec((None, ATTN_HEADS, 1), pick),
        pl.BlockSpec((LANES, 2 * LANES), lambda bi, si: (0, 0)),
    ]
    wide_f32 = jax.ShapeDtypeStruct((b, s, SEG), F32)
    wide_bf16 = jax.ShapeDtypeStruct((b, s, SEG), BF16)
    out_shape = (wide_f32, wide_f32, wide_bf16, wide_bf16, wide_bf16, wide_f32,
                 jax.ShapeDtypeStruct((b, ATTN_HEADS, s), F32))
    out_specs = (pl.BlockSpec((1, tm, SEG), _row),) * 6 + (
        pl.BlockSpec((1, ATTN_HEADS, tm), lambda bi, si: (bi, 0, si)),)
    return in_specs, out_specs, out_shape


def _mix_specs(d, layer):
    tm = ROW_TILE
    wide = pl.BlockSpec((1, tm, SEG), _row)
    return [
        pl.BlockSpec((1, tm, d), _row), wide, wide, wide,
        pl.BlockSpec((None, len(POOL_WINDOWS), POOL_GROUP_DIM, POOL_GROUP_DIM),
                     lambda bi, si: (layer, 0, 0, 0)),
        pl.BlockSpec((None, 1, POOL_WIDTH), lambda bi, si: (layer, 0, 0)),
        pl.BlockSpec((None, d, d), lambda bi, si: (layer, 0, 0)),
    ]


_ROW_PARAMS = pltpu.CompilerParams(dimension_semantics=("arbitrary", "arbitrary"),
                                   vmem_limit_bytes=VMEM_LIMIT)


def _first(x, proj_args):
    b, s, d = x.shape
    in_specs, out_specs, out_shape = _proj_specs(b, s, d, 0)
    return pl.pallas_call(
        _first_kernel,
        grid=(b, s // ROW_TILE),
        in_specs=[pl.BlockSpec((1, ROW_TILE, d), _row)] + in_specs,
        out_specs=out_specs,
        out_shape=out_shape,
        scratch_shapes=[pltpu.VMEM((ATTN_HEADS, LANES), F32)],
        compiler_params=_ROW_PARAMS,
        name="first",
    )(x, *proj_args)


def _middle(x, u, pg, a, mix_args, proj_args, layer):
    b, s, d = x.shape
    in_specs, out_specs, out_shape = _proj_specs(b, s, d, layer + 1)
    return pl.pallas_call(
        _middle_kernel,
        grid=(b, s // ROW_TILE),
        in_specs=_mix_specs(d, layer) + in_specs,
        out_specs=(pl.BlockSpec((1, ROW_TILE, d), _row),) + out_specs,
        out_shape=(jax.ShapeDtypeStruct((b, s, d), F32),) + out_shape,
        scratch_shapes=[pltpu.VMEM((HALO + ROW_TILE, POOL_WIDTH), F32),
                        pltpu.VMEM((ATTN_HEADS, LANES), F32)],
        compiler_params=_ROW_PARAMS,
        name="middle",
    )(x, u, pg, a, *mix_args, *proj_args)


def _last(x, u, pg, a, mix_args, final_g, layer):
    b, s, d = x.shape
    return pl.pallas_call(
        _last_kernel,
        grid=(b, s // ROW_TILE),
        in_specs=_mix_specs(d, layer) + [pl.BlockSpec((1, d), lambda bi, si: (0, 0))],
        out_specs=pl.BlockSpec((1, ROW_TILE, d), _row),
        out_shape=jax.ShapeDtypeStruct((b, s, d), F32),
        scratch_shapes=[pltpu.VMEM((HALO + ROW_TILE, POOL_WIDTH), F32)],
        compiler_params=_ROW_PARAMS,
        name="last",
    )(x, u, pg, a, *mix_args, final_g)


def _attn_kernel(q_ref, k_ref, v_ref, c_ref, ag_ref, o_ref, m_sc, l_sc, acc_sc):
    tq, tk, rb = Q_TILE, KV_TILE, ROW_BLOCK
    per = tq // tk
    blocks = tq // rb
    lane = lax.broadcasted_iota(jnp.int32, (1, LANES), 1)
    first = lane < HEAD_DIM

    def load_tile(j):
        start = pl.multiple_of(j * tk, tk)
        vs = v_ref[0, pl.ds(start, tk), :]
        vs1 = jnp.concatenate([vs, jnp.ones_like(vs)], axis=1)
        bias = [c_ref[0, 0, hh:hh + 1, pl.ds(start, tk)] for hh in range(2)]
        return k_ref[0, pl.ds(start, tk), :], vs1, bias

    def query_tile(qi, carry):
        q_rows = pl.ds(pl.multiple_of(qi * tq, tq), tq)
        q = q_ref[0, q_rows, :]
        zero = jnp.zeros_like(q)
        q0, q1 = jnp.where(first, q, zero), jnp.where(first, zero, q)
        q_units = [jnp.concatenate([q0[r * rb:(r + 1) * rb], q1[r * rb:(r + 1) * rb]], axis=0)
                   for r in range(blocks)]

        m_sc[...] = jnp.full_like(m_sc, NEG_INF)
        l_sc[...] = jnp.zeros_like(l_sc)
        acc_sc[...] = jnp.zeros_like(acc_sc)

        def scores(unit):
            r, n, inputs, _ = unit
            return lax.dot_general(q_units[r], inputs[0][:n], (((1,), (1,)), ((), ())),
                                   preferred_element_type=F32)

        def finish(unit, s2):
            r, n, (_, vs1, bias), col0 = unit
            rows = slice(r * rb, (r + 1) * rb)
            ps, alphas = [], []
            for hh in range(2):
                s = s2[hh * rb:(hh + 1) * rb] - bias[hh][:, :n]
                if col0 is not None:
                    row = r * rb + lax.broadcasted_iota(jnp.int32, (rb, n), 0)
                    col = col0 + lax.broadcasted_iota(jnp.int32, (rb, n), 1)
                    s = jnp.where(row >= col, s, NEG_INF)
                tiles = [s[:, c * LANES:(c + 1) * LANES] for c in range(n // LANES)]
                smax = functools.reduce(jnp.maximum, tiles)
                m_old = m_sc[hh, rows, :]
                m_new = jnp.maximum(m_old, jnp.max(smax, axis=-1, keepdims=True))
                alphas.append(jnp.exp2(m_old - m_new))
                m_sc[hh, rows, :] = m_new
                ps.append(jnp.concatenate([jnp.exp2((x - m_new).astype(BF16)) for x in tiles], axis=1))
            pv = jnp.dot(jnp.concatenate(ps, axis=0), vs1[:n], preferred_element_type=F32)
            for hh in range(2):
                part = pv[hh * rb:(hh + 1) * rb]
                acc_sc[hh, rows, :] = alphas[hh] * acc_sc[hh, rows, :] + part[:, :LANES]
                l_sc[hh, rows, :] = alphas[hh] * l_sc[hh, rows, :] + part[:, LANES:]

        def run(units):
            pending = [scores(u) for u in units[:LOOKAHEAD]]
            for i, unit in enumerate(units):
                if i + LOOKAHEAD < len(units):
                    pending.append(scores(units[i + LOOKAHEAD]))
                finish(unit, pending.pop(0))

        def past_tiles(i, carry):
            units = []
            for d in range(per):
                inputs = load_tile(i * per + d)
                units += [(r, tk, inputs, None) for r in range(blocks)]
            run(units)
            return carry

        lax.fori_loop(0, qi, past_tiles, 0)

        units = []
        for d in range(per):
            inputs = load_tile(qi * per + d)
            for r in range(blocks):
                n = max(0, min(tk, (r + 1) * rb - d * tk))
                if n > 0:
                    causal = r * rb < d * tk + n
                    units.append((r, n, inputs, d * tk if causal else None))
        run(units)

        out = jnp.where(first, acc_sc[0] / l_sc[0], acc_sc[1] / l_sc[1])
        o_ref[0, q_rows, :] = (out * _silu(ag_ref[0, q_rows, :])).astype(BF16)
        return carry

    lax.fori_loop(0, q_ref.shape[1] // tq, query_tile, 0)


def _attn(q, k, v, c, ag):
    b, s, _ = q.shape
    pairs = ATTN_WIDTH // LANES
    full = pl.BlockSpec((1, s, LANES), lambda bi, hp: (bi, 0, hp))
    return pl.pallas_call(
        _attn_kernel,
        grid=(b, pairs),
        in_specs=[full, full, full, pl.BlockSpec((1, 1, 2, s), lambda bi, hp: (bi, hp, 0, 0)), full],
        out_specs=full,
        out_shape=jax.ShapeDtypeStruct((b, s, ATTN_WIDTH), BF16),
        scratch_shapes=[pltpu.VMEM((2, Q_TILE, LANES), F32)] * 3,
        compiler_params=pltpu.CompilerParams(
            dimension_semantics=("arbitrary", "arbitrary"), vmem_limit_bytes=VMEM_LIMIT),
        name="attn",
    )(q, k, v, c.reshape(b, pairs, 2, s), ag)


def kernel(x, norm_g, w_in, forget_bias, pool_w, pool_scale, w_out, final_g):
    w_all = w_in.astype(BF16)
    w_f = jnp.pad(w_all[:, :, N_SEG * SEG:], ((0, 0), (0, 0), (0, LANES - ATTN_HEADS)))
    fb = forget_bias.astype(F32)[:, :, None]
    g = norm_g.astype(F32)[:, None, :]
    idx = jnp.arange(LANES)
    prefix = jnp.concatenate([(idx[:, None] <= idx[None, :]).astype(BF16),
                              jnp.ones((LANES, LANES), BF16)], axis=1)
    proj_args = (g, w_all, w_f, fb, prefix)
    mix_args = (pool_w.astype(BF16), pool_scale.astype(F32)[:, None, :], w_out.astype(BF16))
    fg = final_g.astype(F32)[None, :]

    u, pg, q, k, v, ag, c = _first(x, proj_args)
    for layer in range(DEPTH):
        a = _attn(q, k, v, c, ag)
        if layer + 1 < DEPTH:
            x, u, pg, q, k, v, ag, c = _middle(x, u, pg, a, mix_args, proj_args, layer)
        else:
            x = _last(x, u, pg, a, mix_args, fg, layer)
    return x
```

```python
import functools

import jax
import jax.numpy as jnp
from jax import lax
from jax.experimental import pallas as pl
from jax.experimental.pallas import tpu as pltpu

F32 = jnp.float32
BF16 = jnp.bfloat16

D_MODEL = 1024
DEPTH = 4
POOL_WIDTH = 512
POOL_WINDOWS = (2, 4, 8, 16)
POOL_GROUP_DIM = 128
HEAD_DIM = 64
ATTN_WIDTH = 512
ATTN_HEADS = 8
SEG = 512
N_SEG = 6
RMS_EPS = 1e-6
NEG_INF = -1e30

LANES = 128
HALO = 16
VMEM_LIMIT = 56 * 1024 * 1024

ROW_TILE = 512
Q_TILE = 1024
KV_TILE = 512
ROW_BLOCK = 256
LOOKAHEAD = 1
LOG2E = 1.4426950408889634


def _silu(g):
    return g * (1.0 / (1.0 + jnp.exp(-g)))


def _log_sigmoid(z):
    return jnp.minimum(z, 0.0) - jnp.log1p(jnp.exp(-jnp.abs(z)))


def _pool(u, pg, pw_ref, ps_ref, ext_ref):
    si = pl.program_id(1)
    tm = u.shape[0]
    ext_ref[HALO:, :] = u
    pos = si * tm + lax.broadcasted_iota(jnp.int32, (tm, 1), 0)
    parts = []
    for gi, w in enumerate(POOL_WINDOWS):
        cols = slice(gi * POOL_GROUP_DIM, (gi + 1) * POOL_GROUP_DIM)
        win = ext_ref[:, cols]
        for st in range(gi + 1):
            win = win + pltpu.roll(win, 1 << st, axis=0)
        count = jnp.minimum(pos + 1, w).astype(F32)
        dlt = win[HALO:] / count - u[:, cols]
        parts.append(jnp.dot(dlt.astype(BF16), pw_ref[gi], preferred_element_type=F32))
    ext_ref[0:HALO, :] = u[tm - HALO:]
    y = jnp.concatenate(parts, axis=1) * ps_ref[...]
    return (y * _silu(pg)).astype(BF16)


def _project(x, g_ref, w_ref, wf_ref, fb_ref, prefix_ref, pw_ref, ps_ref,
             po_ref, q_ref, k_ref, v_ref, ag_ref, c_ref, carry_ref, ext_ref):
    @pl.when(pl.program_id(1) == 0)
    def _():
        carry_ref[...] = jnp.zeros_like(carry_ref)
        ext_ref[0:HALO, :] = jnp.zeros((HALO, POOL_WIDTH), F32)

    ms = jnp.mean(x * x, axis=-1, keepdims=True)
    h = (x * lax.rsqrt(ms + RMS_EPS) * g_ref[...]).astype(BF16)

    def seg(i):
        return jnp.dot(h, w_ref[:, i * SEG:(i + 1) * SEG], preferred_element_type=F32)

    z = jnp.dot(h, wf_ref[...], preferred_element_type=F32)
    po_ref[0] = _pool(seg(0), seg(1), pw_ref, ps_ref, ext_ref)
    q_ref[0] = (seg(2) * (HEAD_DIM ** -0.5 * LOG2E)).astype(BF16)
    k_ref[0] = seg(3).astype(BF16)

    lf = _log_sigmoid(z.T[:ATTN_HEADS] + fb_ref[...])
    tm = lf.shape[1]
    hi = lf.astype(BF16).astype(F32)
    rest = lf - hi
    mid = rest.astype(BF16).astype(F32)
    parts = jnp.concatenate([hi, mid, rest - mid, jnp.zeros_like(lf)], axis=0).astype(BF16)
    carry = carry_ref[...]
    for j in range(tm // LANES):
        res = jnp.dot(parts[:, j * LANES:(j + 1) * LANES], prefix_ref[...], preferred_element_type=F32)
        res = res[0:8] + res[8:16] + res[16:24]
        c_ref[0, :, j * LANES:(j + 1) * LANES] = (res[:, :LANES] + carry) * LOG2E
        carry = carry + res[:, LANES:]
    carry_ref[...] = carry

    v_ref[0] = seg(4).astype(BF16)
    ag_ref[0] = seg(5)


def _mix(x_ref, po_ref, a_ref, wo_ref):
    acc = jnp.dot(po_ref[0], wo_ref[0:POOL_WIDTH, :], preferred_element_type=F32)
    acc = acc + jnp.dot(a_ref[0], wo_ref[POOL_WIDTH:, :], preferred_element_type=F32)
    return x_ref[0] + acc


N_PROJ_IN = 7
N_PROJ_OUT = 6


def _first_kernel(x_ref, *refs):
    _project(x_ref[0], *refs)


def _middle_kernel(x_ref, po_ref, a_ref, wo_ref, *refs):
    proj_in, (xo_ref, *proj_out) = refs[:N_PROJ_IN], refs[N_PROJ_IN:N_PROJ_IN + 1 + N_PROJ_OUT]
    scratch = refs[N_PROJ_IN + 1 + N_PROJ_OUT:]
    xn = _mix(x_ref, po_ref, a_ref, wo_ref)
    xo_ref[0] = xn
    _project(xn, *proj_in, *proj_out, *scratch)


def _last_kernel(x_ref, po_ref, a_ref, wo_ref, fg_ref, o_ref):
    xn = _mix(x_ref, po_ref, a_ref, wo_ref)
    ms = jnp.mean(xn * xn, axis=-1, keepdims=True)
    o_ref[0] = xn * lax.rsqrt(ms + RMS_EPS) * fg_ref[...]


def _row(bi, si):
    return (bi, si, 0)


def _proj_specs(b, s, d, layer):
    tm = ROW_TILE
    pick = lambda bi, si: (layer, 0, 0)
    wide = N_SEG * SEG
    in_specs = [
        pl.BlockSpec((None, 1, d), pick),
        pl.BlockSpec((None, d, wide), pick),
        pl.BlockSpec((None, d, LANES), pick),
        pl.BlockSpec((None, ATTN_HEADS, 1), pick),
        pl.BlockSpec((LANES, 2 * LANES), lambda bi, si: (0, 0)),
        pl.BlockSpec((None, len(POOL_WINDOWS), POOL_GROUP_DIM, POOL_GROUP_DIM),
                     lambda bi, si: (layer, 0, 0, 0)),
        pl.BlockSpec((None, 1, POOL_WIDTH), pick),
    ]
    wide_f32 = jax.ShapeDtypeStruct((b, s, SEG), F32)
    wide_bf16 = jax.ShapeDtypeStruct((b, s, SEG), BF16)
    out_shape = (wide_bf16, wide_bf16, wide_bf16, wide_bf16, wide_f32,
                 jax.ShapeDtypeStruct((b, ATTN_HEADS, s), F32))
    out_specs = (pl.BlockSpec((1, tm, SEG), _row),) * 5 + (
        pl.BlockSpec((1, ATTN_HEADS, tm), lambda bi, si: (bi, 0, si)),)
    scratch = [pltpu.VMEM((ATTN_HEADS, LANES), F32), pltpu.VMEM((HALO + tm, POOL_WIDTH), F32)]
    assert len(in_specs) == N_PROJ_IN and len(out_specs) == N_PROJ_OUT
    return in_specs, out_specs, out_shape, scratch


def _mix_specs(d, layer):
    tm = ROW_TILE
    wide = pl.BlockSpec((1, tm, SEG), _row)
    return [pl.BlockSpec((1, tm, d), _row), wide, wide,
            pl.BlockSpec((None, d, d), lambda bi, si: (layer, 0, 0))]


_ROW_PARAMS = pltpu.CompilerParams(dimension_semantics=("arbitrary", "arbitrary"),
                                   vmem_limit_bytes=VMEM_LIMIT)


def _first(x, proj_args):
    b, s, d = x.shape
    in_specs, out_specs, out_shape, scratch = _proj_specs(b, s, d, 0)
    return pl.pallas_call(
        _first_kernel,
        grid=(b, s // ROW_TILE),
        in_specs=[pl.BlockSpec((1, ROW_TILE, d), _row)] + in_specs,
        out_specs=out_specs,
        out_shape=out_shape,
        scratch_shapes=scratch,
        compiler_params=_ROW_PARAMS,
        name="first",
    )(x, *proj_args)


def _middle(x, po, a, w_out, proj_args, layer):
    b, s, d = x.shape
    in_specs, out_specs, out_shape, scratch = _proj_specs(b, s, d, layer + 1)
    return pl.pallas_call(
        _middle_kernel,
        grid=(b, s // ROW_TILE),
        in_specs=_mix_specs(d, layer) + in_specs,
        out_specs=(pl.BlockSpec((1, ROW_TILE, d), _row),) + out_specs,
        out_shape=(jax.ShapeDtypeStruct((b, s, d), F32),) + out_shape,
        scratch_shapes=scratch,
        compiler_params=_ROW_PARAMS,
        name="middle",
    )(x, po, a, w_out, *proj_args)


def _last(x, po, a, w_out, final_g, layer):
    b, s, d = x.shape
    return pl.pallas_call(
        _last_kernel,
        grid=(b, s // ROW_TILE),
        in_specs=_mix_specs(d, layer) + [pl.BlockSpec((1, d), lambda bi, si: (0, 0))],
        out_specs=pl.BlockSpec((1, ROW_TILE, d), _row),
        out_shape=jax.ShapeDtypeStruct((b, s, d), F32),
        compiler_params=_ROW_PARAMS,
        name="last",
    )(x, po, a, w_out, final_g)


def _attn_kernel(q_ref, k_ref, v_ref, c_ref, ag_ref, o_ref, m_sc, l_sc, acc_sc):
    tq, tk, rb = Q_TILE, KV_TILE, ROW_BLOCK
    per = tq // tk
    blocks = tq // rb
    lane = lax.broadcasted_iota(jnp.int32, (1, LANES), 1)
    first = lane < HEAD_DIM

    def load_tile(j):
        start = pl.multiple_of(j * tk, tk)
        vs = v_ref[0, pl.ds(start, tk), :]
        vs1 = jnp.concatenate([vs, jnp.ones_like(vs)], axis=1)
        bias = [c_ref[0, 0, hh:hh + 1, pl.ds(start, tk)] for hh in range(2)]
        return k_ref[0, pl.ds(start, tk), :], vs1, bias

    def query_tile(qi, carry):
        q_rows = pl.ds(pl.multiple_of(qi * tq, tq), tq)
        q = q_ref[0, q_rows, :]
        zero = jnp.zeros_like(q)
        q0, q1 = jnp.where(first, q, zero), jnp.where(first, zero, q)
        q_units = [jnp.concatenate([q0[r * rb:(r + 1) * rb], q1[r * rb:(r + 1) * rb]], axis=0)
                   for r in range(blocks)]

        m_sc[...] = jnp.full_like(m_sc, NEG_INF)
        l_sc[...] = jnp.zeros_like(l_sc)
        acc_sc[...] = jnp.zeros_like(acc_sc)

        def scores(unit):
            r, n, inputs, _ = unit
            return lax.dot_general(q_units[r], inputs[0][:n], (((1,), (1,)), ((), ())),
                                   preferred_element_type=F32)

        def finish(unit, s2):
            r, n, (_, vs1, bias), col0 = unit
            rows = slice(r * rb, (r + 1) * rb)
            ps, alphas = [], []
            for hh in range(2):
                s = s2[hh * rb:(hh + 1) * rb] - bias[hh][:, :n]
                if col0 is not None:
                    row = r * rb + lax.broadcasted_iota(jnp.int32, (rb, n), 0)
                    col = col0 + lax.broadcasted_iota(jnp.int32, (rb, n), 1)
                    s = jnp.where(row >= col, s, NEG_INF)
                tiles = [s[:, c * LANES:(c + 1) * LANES] for c in range(n // LANES)]
                smax = functools.reduce(jnp.maximum, tiles)
                m_old = m_sc[hh, rows, :]
                m_new = jnp.maximum(m_old, jnp.max(smax, axis=-1, keepdims=True))
                alphas.append(jnp.exp2(m_old - m_new))
                m_sc[hh, rows, :] = m_new
                ps.append(jnp.concatenate([jnp.exp2((x - m_new).astype(BF16)) for x in tiles], axis=1))
            pv = jnp.dot(jnp.concatenate(ps, axis=0), vs1[:n], preferred_element_type=F32)
            for hh in range(2):
                part = pv[hh * rb:(hh + 1) * rb]
                acc_sc[hh, rows, :] = alphas[hh] * acc_sc[hh, rows, :] + part[:, :LANES]
                l_sc[hh, rows, :] = alphas[hh] * l_sc[hh, rows, :] + part[:, LANES:]

        def run(units):
            pending = [scores(u) for u in units[:LOOKAHEAD]]
            for i, unit in enumerate(units):
                if i + LOOKAHEAD < len(units):
                    pending.append(scores(units[i + LOOKAHEAD]))
                finish(unit, pending.pop(0))

        def past_tiles(i, carry):
            units = []
            for d in range(per):
                inputs = load_tile(i * per + d)
                units += [(r, tk, inputs, None) for r in range(blocks)]
            run(units)
            return carry

        lax.fori_loop(0, qi, past_tiles, 0)

        units = []
        for d in range(per):
            inputs = load_tile(qi * per + d)
            for r in range(blocks):
                n = max(0, min(tk, (r + 1) * rb - d * tk))
                if n > 0:
                    causal = r * rb < d * tk + n
                    units.append((r, n, inputs, d * tk if causal else None))
        run(units)

        out = jnp.where(first, acc_sc[0] / l_sc[0], acc_sc[1] / l_sc[1])
        o_ref[0, q_rows, :] = (out * _silu(ag_ref[0, q_rows, :])).astype(BF16)
        return carry

    lax.fori_loop(0, q_ref.shape[1] // tq, query_tile, 0)


def _attn(q, k, v, c, ag):
    b, s, _ = q.shape
    pairs = ATTN_WIDTH // LANES
    full = pl.BlockSpec((1, s, LANES), lambda bi, hp: (bi, 0, hp))
    return pl.pallas_call(
        _attn_kernel,
        grid=(b, pairs),
        in_specs=[full, full, full, pl.BlockSpec((1, 1, 2, s), lambda bi, hp: (bi, hp, 0, 0)), full],
        out_specs=full,
        out_shape=jax.ShapeDtypeStruct((b, s, ATTN_WIDTH), BF16),
        scratch_shapes=[pltpu.VMEM((2, Q_TILE, LANES), F32)] * 3,
        compiler_params=pltpu.CompilerParams(
            dimension_semantics=("arbitrary", "arbitrary"), vmem_limit_bytes=VMEM_LIMIT),
        name="attn",
    )(q, k, v, c.reshape(b, pairs, 2, s), ag)


def kernel(x, norm_g, w_in, forget_bias, pool_w, pool_scale, w_out, final_g):
    w_all = w_in.astype(BF16)
    w_f = jnp.pad(w_all[:, :, N_SEG * SEG:], ((0, 0), (0, 0), (0, LANES - ATTN_HEADS)))
    fb = forget_bias.astype(F32)[:, :, None]
    g = norm_g.astype(F32)[:, None, :]
    idx = jnp.arange(LANES)
    prefix = jnp.concatenate([(idx[:, None] <= idx[None, :]).astype(BF16),
                              jnp.ones((LANES, LANES), BF16)], axis=1)
    proj_args = (g, w_all, w_f, fb, prefix, pool_w.astype(BF16), pool_scale.astype(F32)[:, None, :])
    wo = w_out.astype(BF16)
    fg = final_g.astype(F32)[None, :]

    po, q, k, v, ag, c = _first(x, proj_args)
    for layer in range(DEPTH):
        a = _attn(q, k, v, c, ag)
        if layer + 1 < DEPTH:
            x, po, q, k, v, ag, c = _middle(x, po, a, wo, proj_args, layer)
        else:
            x = _last(x, po, a, wo, fg, layer)
    return x
```

```python
import functools

import jax
import jax.numpy as jnp
from jax import lax
from jax.experimental import pallas as pl
from jax.experimental.pallas import tpu as pltpu

F32 = jnp.float32
BF16 = jnp.bfloat16

D_MODEL = 1024
DEPTH = 4
POOL_WIDTH = 512
POOL_WINDOWS = (2, 4, 8, 16)
POOL_GROUP_DIM = 128
HEAD_DIM = 64
ATTN_WIDTH = 512
ATTN_HEADS = 8
SEG = 512
N_SEG = 6
RMS_EPS = 1e-6
NEG_INF = -1e30

LANES = 128
HALO = 16
VMEM_LIMIT = 56 * 1024 * 1024

ROW_TILE = 1024
Q_TILE = 1024
KV_TILE = 512
ROW_BLOCK = 256
LOOKAHEAD = 1
LOG2E = 1.4426950408889634


def _silu(g):
    return g * (1.0 / (1.0 + jnp.exp(-g)))


def _log_sigmoid(z):
    return jnp.minimum(z, 0.0) - jnp.log1p(jnp.exp(-jnp.abs(z)))


def _pool(u, pg, pw_ref, ps_ref, ext_ref):
    si = pl.program_id(1)
    tm = u.shape[0]
    ext_ref[HALO:, :] = u
    pos = si * tm + lax.broadcasted_iota(jnp.int32, (tm, 1), 0)
    parts = []
    for gi, w in enumerate(POOL_WINDOWS):
        cols = slice(gi * POOL_GROUP_DIM, (gi + 1) * POOL_GROUP_DIM)
        win = ext_ref[:, cols]
        for st in range(gi + 1):
            win = win + pltpu.roll(win, 1 << st, axis=0)
        count = jnp.minimum(pos + 1, w).astype(F32)
        dlt = win[HALO:] / count - u[:, cols]
        parts.append(jnp.dot(dlt.astype(BF16), pw_ref[gi], preferred_element_type=F32))
    ext_ref[0:HALO, :] = u[tm - HALO:]
    y = jnp.concatenate(parts, axis=1) * ps_ref[...]
    return (y * _silu(pg)).astype(BF16)


def _project(x, g_ref, w_ref, wf_ref, fb_ref, prefix_ref, pw_ref, ps_ref,
             po_ref, q_ref, k_ref, v_ref, ag_ref, c_ref, carry_ref, ext_ref):
    @pl.when(pl.program_id(1) == 0)
    def _():
        carry_ref[...] = jnp.zeros_like(carry_ref)
        ext_ref[0:HALO, :] = jnp.zeros((HALO, POOL_WIDTH), F32)

    ms = jnp.mean(x * x, axis=-1, keepdims=True)
    h = (x * lax.rsqrt(ms + RMS_EPS) * g_ref[...]).astype(BF16)

    def seg(i):
        return jnp.dot(h, w_ref[:, i * SEG:(i + 1) * SEG], preferred_element_type=F32)

    z = jnp.dot(h, wf_ref[...], preferred_element_type=F32)
    po_ref[0] = _pool(seg(0), seg(1), pw_ref, ps_ref, ext_ref)
    q_ref[0] = (seg(2) * (HEAD_DIM ** -0.5 * LOG2E)).astype(BF16)
    k_ref[0] = seg(3).astype(BF16)

    lf = _log_sigmoid(z.T[:ATTN_HEADS] + fb_ref[...])
    tm = lf.shape[1]
    hi = lf.astype(BF16).astype(F32)
    rest = lf - hi
    mid = rest.astype(BF16).astype(F32)
    parts = jnp.concatenate([hi, mid, rest - mid, jnp.zeros_like(lf)], axis=0).astype(BF16)
    carry = carry_ref[...]
    for j in range(tm // LANES):
        res = jnp.dot(parts[:, j * LANES:(j + 1) * LANES], prefix_ref[...], preferred_element_type=F32)
        res = res[0:8] + res[8:16] + res[16:24]
        c_ref[0, :, j * LANES:(j + 1) * LANES] = (res[:, :LANES] + carry) * LOG2E
        carry = carry + res[:, LANES:]
    carry_ref[...] = carry

    v_ref[0] = seg(4).astype(BF16)
    ag_ref[0] = seg(5)


def _mix(x_ref, po_ref, a_ref, wo_ref):
    acc = jnp.dot(po_ref[0], wo_ref[0:POOL_WIDTH, :], preferred_element_type=F32)
    acc = acc + jnp.dot(a_ref[0], wo_ref[POOL_WIDTH:, :], preferred_element_type=F32)
    return x_ref[0] + acc


N_PROJ_IN = 7
N_PROJ_OUT = 6


def _first_kernel(x_ref, *refs):
    _project(x_ref[0], *refs)


def _middle_kernel(x_ref, po_ref, a_ref, wo_ref, *refs):
    proj_in, (xo_ref, *proj_out) = refs[:N_PROJ_IN], refs[N_PROJ_IN:N_PROJ_IN + 1 + N_PROJ_OUT]
    scratch = refs[N_PROJ_IN + 1 + N_PROJ_OUT:]
    xn = _mix(x_ref, po_ref, a_ref, wo_ref)
    xo_ref[0] = xn
    _project(xn, *proj_in, *proj_out, *scratch)


def _last_kernel(x_ref, po_ref, a_ref, wo_ref, fg_ref, o_ref):
    xn = _mix(x_ref, po_ref, a_ref, wo_ref)
    ms = jnp.mean(xn * xn, axis=-1, keepdims=True)
    o_ref[0] = xn * lax.rsqrt(ms + RMS_EPS) * fg_ref[...]


def _row(bi, si):
    return (bi, si, 0)


def _proj_specs(b, s, d, layer):
    tm = ROW_TILE
    pick = lambda bi, si: (layer, 0, 0)
    wide = N_SEG * SEG
    in_specs = [
        pl.BlockSpec((None, 1, d), pick),
        pl.BlockSpec((None, d, wide), pick, pipeline_mode=pl.Buffered(1)),
        pl.BlockSpec((None, d, LANES), pick),
        pl.BlockSpec((None, ATTN_HEADS, 1), pick),
        pl.BlockSpec((LANES, 2 * LANES), lambda bi, si: (0, 0)),
        pl.BlockSpec((None, len(POOL_WINDOWS), POOL_GROUP_DIM, POOL_GROUP_DIM),
                     lambda bi, si: (layer, 0, 0, 0)),
        pl.BlockSpec((None, 1, POOL_WIDTH), pick),
    ]
    wide_f32 = jax.ShapeDtypeStruct((b, s, SEG), F32)
    wide_bf16 = jax.ShapeDtypeStruct((b, s, SEG), BF16)
    out_shape = (wide_bf16, wide_bf16, wide_bf16, wide_bf16, wide_f32,
                 jax.ShapeDtypeStruct((b, ATTN_HEADS, s), F32))
    out_specs = (pl.BlockSpec((1, tm, SEG), _row),) * 5 + (
        pl.BlockSpec((1, ATTN_HEADS, tm), lambda bi, si: (bi, 0, si)),)
    scratch = [pltpu.VMEM((ATTN_HEADS, LANES), F32), pltpu.VMEM((HALO + tm, POOL_WIDTH), F32)]
    assert len(in_specs) == N_PROJ_IN and len(out_specs) == N_PROJ_OUT
    return in_specs, out_specs, out_shape, scratch


def _mix_specs(d, layer):
    tm = ROW_TILE
    wide = pl.BlockSpec((1, tm, SEG), _row)
    return [pl.BlockSpec((1, tm, d), _row), wide, wide,
            pl.BlockSpec((None, d, d), lambda bi, si: (layer, 0, 0), pipeline_mode=pl.Buffered(1))]


_ROW_PARAMS = pltpu.CompilerParams(dimension_semantics=("arbitrary", "arbitrary"),
                                   vmem_limit_bytes=VMEM_LIMIT)


def _first(x, proj_args):
    b, s, d = x.shape
    in_specs, out_specs, out_shape, scratch = _proj_specs(b, s, d, 0)
    return pl.pallas_call(
        _first_kernel,
        grid=(b, s // ROW_TILE),
        in_specs=[pl.BlockSpec((1, ROW_TILE, d), _row)] + in_specs,
        out_specs=out_specs,
        out_shape=out_shape,
        scratch_shapes=scratch,
        compiler_params=_ROW_PARAMS,
        name="first",
    )(x, *proj_args)


def _middle(x, po, a, w_out, proj_args, layer):
    b, s, d = x.shape
    in_specs, out_specs, out_shape, scratch = _proj_specs(b, s, d, layer + 1)
    return pl.pallas_call(
        _middle_kernel,
        grid=(b, s // ROW_TILE),
        in_specs=_mix_specs(d, layer) + in_specs,
        out_specs=(pl.BlockSpec((1, ROW_TILE, d), _row),) + out_specs,
        out_shape=(jax.ShapeDtypeStruct((b, s, d), F32),) + out_shape,
        scratch_shapes=scratch,
        compiler_params=_ROW_PARAMS,
        name="middle",
    )(x, po, a, w_out, *proj_args)


def _last(x, po, a, w_out, final_g, layer):
    b, s, d = x.shape
    return pl.pallas_call(
        _last_kernel,
        grid=(b, s // ROW_TILE),
        in_specs=_mix_specs(d, layer) + [pl.BlockSpec((1, d), lambda bi, si: (0, 0))],
        out_specs=pl.BlockSpec((1, ROW_TILE, d), _row),
        out_shape=jax.ShapeDtypeStruct((b, s, d), F32),
        compiler_params=_ROW_PARAMS,
        name="last",
    )(x, po, a, w_out, final_g)


def _attn_kernel(q_ref, k_ref, v_ref, c_ref, ag_ref, o_ref, m_sc, l_sc, acc_sc):
    tq, tk, rb = Q_TILE, KV_TILE, ROW_BLOCK
    per = tq // tk
    blocks = tq // rb
    lane = lax.broadcasted_iota(jnp.int32, (1, LANES), 1)
    first = lane < HEAD_DIM

    def load_tile(j):
        start = pl.multiple_of(j * tk, tk)
        vs = v_ref[0, pl.ds(start, tk), :]
        vs1 = jnp.concatenate([vs, jnp.ones_like(vs)], axis=1)
        bias = [c_ref[0, pl.ds(2 * pl.program_id(1) + hh, 1), pl.ds(start, tk)] for hh in range(2)]
        return k_ref[0, pl.ds(start, tk), :], vs1, bias

    def query_tile(qi, carry):
        q_rows = pl.ds(pl.multiple_of(qi * tq, tq), tq)
        q = q_ref[0, q_rows, :]
        zero = jnp.zeros_like(q)
        q0, q1 = jnp.where(first, q, zero), jnp.where(first, zero, q)
        q_units = [jnp.concatenate([q0[r * rb:(r + 1) * rb], q1[r * rb:(r + 1) * rb]], axis=0)
                   for r in range(blocks)]

        m_sc[...] = jnp.full_like(m_sc, NEG_INF)
        l_sc[...] = jnp.zeros_like(l_sc)
        acc_sc[...] = jnp.zeros_like(acc_sc)

        def scores(unit):
            r, n, inputs, _ = unit
            return lax.dot_general(q_units[r], inputs[0][:n], (((1,), (1,)), ((), ())),
                                   preferred_element_type=F32)

        def finish(unit, s2):
            r, n, (_, vs1, bias), col0 = unit
            rows = slice(r * rb, (r + 1) * rb)
            ps, alphas = [], []
            for hh in range(2):
                s = s2[hh * rb:(hh + 1) * rb] - bias[hh][:, :n]
                if col0 is not None:
                    row = r * rb + lax.broadcasted_iota(jnp.int32, (rb, n), 0)
                    col = col0 + lax.broadcasted_iota(jnp.int32, (rb, n), 1)
                    s = jnp.where(row >= col, s, NEG_INF)
                tiles = [s[:, c * LANES:(c + 1) * LANES] for c in range(n // LANES)]
                smax = functools.reduce(jnp.maximum, tiles)
                m_old = m_sc[hh, rows, :]
                m_new = jnp.maximum(m_old, jnp.max(smax, axis=-1, keepdims=True))
                alphas.append(jnp.exp2(m_old - m_new))
                m_sc[hh, rows, :] = m_new
                ps.append(jnp.concatenate([jnp.exp2((x - m_new).astype(BF16)) for x in tiles], axis=1))
            pv = jnp.dot(jnp.concatenate(ps, axis=0), vs1[:n], preferred_element_type=F32)
            for hh in range(2):
                part = pv[hh * rb:(hh + 1) * rb]
                acc_sc[hh, rows, :] = alphas[hh] * acc_sc[hh, rows, :] + part[:, :LANES]
                l_sc[hh, rows, :] = alphas[hh] * l_sc[hh, rows, :] + part[:, LANES:]

        def run(units):
            pending = [scores(u) for u in units[:LOOKAHEAD]]
            for i, unit in enumerate(units):
                if i + LOOKAHEAD < len(units):
                    pending.append(scores(units[i + LOOKAHEAD]))
                finish(unit, pending.pop(0))

        def past_tiles(i, carry):
            units = []
            for d in range(per):
                inputs = load_tile(i * per + d)
                units += [(r, tk, inputs, None) for r in range(blocks)]
            run(units)
            return carry

        lax.fori_loop(0, qi, past_tiles, 0)

        units = []
        for d in range(per):
            inputs = load_tile(qi * per + d)
            for r in range(blocks):
                n = max(0, min(tk, (r + 1) * rb - d * tk))
                if n > 0:
                    causal = r * rb < d * tk + n
                    units.append((r, n, inputs, d * tk if causal else None))
        run(units)

        out = jnp.where(first, acc_sc[0] / l_sc[0], acc_sc[1] / l_sc[1])
        o_ref[0, q_rows, :] = (out * _silu(ag_ref[0, q_rows, :])).astype(BF16)
        return carry

    lax.fori_loop(0, q_ref.shape[1] // tq, query_tile, 0)


def _attn(q, k, v, c, ag):
    b, s, _ = q.shape
    pairs = ATTN_WIDTH // LANES
    full = pl.BlockSpec((1, s, LANES), lambda bi, hp: (bi, 0, hp))
    return pl.pallas_call(
        _attn_kernel,
        grid=(b, pairs),
        in_specs=[full, full, full, pl.BlockSpec((1, ATTN_HEADS, s), lambda bi, hp: (bi, 0, 0)), full],
        out_specs=full,
        out_shape=jax.ShapeDtypeStruct((b, s, ATTN_WIDTH), BF16),
        scratch_shapes=[pltpu.VMEM((2, Q_TILE, LANES), F32)] * 3,
        compiler_params=pltpu.CompilerParams(
            dimension_semantics=("arbitrary", "arbitrary"), vmem_limit_bytes=VMEM_LIMIT),
        name="attn",
    )(q, k, v, c, ag)


def kernel(x, norm_g, w_in, forget_bias, pool_w, pool_scale, w_out, final_g):
    w_all = w_in.astype(BF16)
    w_f = jnp.pad(w_all[:, :, N_SEG * SEG:], ((0, 0), (0, 0), (0, LANES - ATTN_HEADS)))
    fb = forget_bias.astype(F32)[:, :, None]
    g = norm_g.astype(F32)[:, None, :]
    idx = jnp.arange(LANES)
    prefix = jnp.concatenate([(idx[:, None] <= idx[None, :]).astype(BF16),
                              jnp.ones((LANES, LANES), BF16)], axis=1)
    proj_args = (g, w_all, w_f, fb, prefix, pool_w.astype(BF16), pool_scale.astype(F32)[:, None, :])
    wo = w_out.astype(BF16)
    fg = final_g.astype(F32)[None, :]

    po, q, k, v, ag, c = _first(x, proj_args)
    for layer in range(DEPTH):
        a = _attn(q, k, v, c, ag)
        if layer + 1 < DEPTH:
            x, po, q, k, v, ag, c = _middle(x, po, a, wo, proj_args, layer)
        else:
            x = _last(x, po, a, wo, fg, layer)
    return x
```

```python
import functools

import jax
import jax.numpy as jnp
from jax import lax
from jax.experimental import pallas as pl
from jax.experimental.pallas import tpu as pltpu

F32 = jnp.float32
BF16 = jnp.bfloat16

D_MODEL = 1024
DEPTH = 4
POOL_WIDTH = 512
POOL_WINDOWS = (2, 4, 8, 16)
POOL_GROUP_DIM = 128
HEAD_DIM = 64
ATTN_WIDTH = 512
ATTN_HEADS = 8
SEG = 512
N_SEG = 6
RMS_EPS = 1e-6
NEG_INF = -1e30

LANES = 128
HALO = 16
VMEM_LIMIT = 56 * 1024 * 1024

ROW_TILE = 1024
Q_TILE = 2048
KV_TILE = 512
ROW_BLOCK = 256
LOOKAHEAD = 1
LOG2E = 1.4426950408889634


def _silu(g):
    return g * (1.0 / (1.0 + jnp.exp(-g)))


def _log_sigmoid(z):
    return jnp.minimum(z, 0.0) - jnp.log1p(jnp.exp(-jnp.abs(z)))


def _pool(u, pg, pw_ref, ps_ref, ext_ref):
    si = pl.program_id(1)
    tm = u.shape[0]
    ext_ref[HALO:, :] = u
    pos = si * tm + lax.broadcasted_iota(jnp.int32, (tm, 1), 0)
    parts = []
    for gi, w in enumerate(POOL_WINDOWS):
        cols = slice(gi * POOL_GROUP_DIM, (gi + 1) * POOL_GROUP_DIM)
        win = ext_ref[:, cols]
        for st in range(gi + 1):
            win = win + pltpu.roll(win, 1 << st, axis=0)
        count = jnp.minimum(pos + 1, w).astype(F32)
        dlt = win[HALO:] / count - u[:, cols]
        parts.append(jnp.dot(dlt.astype(BF16), pw_ref[gi], preferred_element_type=F32))
    ext_ref[0:HALO, :] = u[tm - HALO:]
    y = jnp.concatenate(parts, axis=1) * ps_ref[...]
    return (y * _silu(pg)).astype(BF16)


def _project(x, g_ref, w_ref, wf_ref, fb_ref, prefix_ref, pw_ref, ps_ref,
             po_ref, q_ref, k_ref, v_ref, ag_ref, c_ref, carry_ref, ext_ref):
    @pl.when(pl.program_id(1) == 0)
    def _():
        carry_ref[...] = jnp.zeros_like(carry_ref)
        ext_ref[0:HALO, :] = jnp.zeros((HALO, POOL_WIDTH), F32)

    ms = jnp.mean(x * x, axis=-1, keepdims=True)
    h = (x * lax.rsqrt(ms + RMS_EPS) * g_ref[...]).astype(BF16)

    def seg(i):
        return jnp.dot(h, w_ref[:, i * SEG:(i + 1) * SEG], preferred_element_type=F32)

    z = jnp.dot(h, wf_ref[...], preferred_element_type=F32)
    po_ref[0] = _pool(seg(0), seg(1), pw_ref, ps_ref, ext_ref)
    q_ref[0] = (seg(2) * (HEAD_DIM ** -0.5 * LOG2E)).astype(BF16)
    k_ref[0] = seg(3).astype(BF16)

    lf = _log_sigmoid(z.T[:ATTN_HEADS] + fb_ref[...])
    tm = lf.shape[1]
    hi = lf.astype(BF16).astype(F32)
    rest = lf - hi
    mid = rest.astype(BF16).astype(F32)
    parts = jnp.concatenate([hi, mid, rest - mid, jnp.zeros_like(lf)], axis=0).astype(BF16)
    carry = carry_ref[...]
    for j in range(tm // LANES):
        res = jnp.dot(parts[:, j * LANES:(j + 1) * LANES], prefix_ref[...], preferred_element_type=F32)
        res = res[0:8] + res[8:16] + res[16:24]
        c_ref[0, :, j * LANES:(j + 1) * LANES] = (res[:, :LANES] + carry) * LOG2E
        carry = carry + res[:, LANES:]
    carry_ref[...] = carry

    v_ref[0] = seg(4).astype(BF16)
    ag_ref[0] = seg(5)


def _mix(x_ref, po_ref, a_ref, wo_ref):
    acc = jnp.dot(po_ref[0], wo_ref[0:POOL_WIDTH, :], preferred_element_type=F32)
    acc = acc + jnp.dot(a_ref[0], wo_ref[POOL_WIDTH:, :], preferred_element_type=F32)
    return x_ref[0] + acc


N_PROJ_IN = 7
N_PROJ_OUT = 6


def _first_kernel(x_ref, *refs):
    _project(x_ref[0], *refs)


def _middle_kernel(x_ref, po_ref, a_ref, wo_ref, *refs):
    proj_in, (xo_ref, *proj_out) = refs[:N_PROJ_IN], refs[N_PROJ_IN:N_PROJ_IN + 1 + N_PROJ_OUT]
    scratch = refs[N_PROJ_IN + 1 + N_PROJ_OUT:]
    xn = _mix(x_ref, po_ref, a_ref, wo_ref)
    xo_ref[0] = xn
    _project(xn, *proj_in, *proj_out, *scratch)


def _last_kernel(x_ref, po_ref, a_ref, wo_ref, fg_ref, o_ref):
    xn = _mix(x_ref, po_ref, a_ref, wo_ref)
    ms = jnp.mean(xn * xn, axis=-1, keepdims=True)
    o_ref[0] = xn * lax.rsqrt(ms + RMS_EPS) * fg_ref[...]


def _row(bi, si):
    return (bi, si, 0)


def _proj_specs(b, s, d, layer):
    tm = ROW_TILE
    pick = lambda bi, si: (layer, 0, 0)
    wide = N_SEG * SEG
    in_specs = [
        pl.BlockSpec((None, 1, d), pick),
        pl.BlockSpec((None, d, wide), pick, pipeline_mode=pl.Buffered(1)),
        pl.BlockSpec((None, d, LANES), pick),
        pl.BlockSpec((None, ATTN_HEADS, 1), pick),
        pl.BlockSpec((LANES, 2 * LANES), lambda bi, si: (0, 0)),
        pl.BlockSpec((None, len(POOL_WINDOWS), POOL_GROUP_DIM, POOL_GROUP_DIM),
                     lambda bi, si: (layer, 0, 0, 0)),
        pl.BlockSpec((None, 1, POOL_WIDTH), pick),
    ]
    wide_f32 = jax.ShapeDtypeStruct((b, s, SEG), F32)
    wide_bf16 = jax.ShapeDtypeStruct((b, s, SEG), BF16)
    out_shape = (wide_bf16, wide_bf16, wide_bf16, wide_bf16, wide_f32,
                 jax.ShapeDtypeStruct((b, ATTN_HEADS, s), F32))
    out_specs = (pl.BlockSpec((1, tm, SEG), _row),) * 5 + (
        pl.BlockSpec((1, ATTN_HEADS, tm), lambda bi, si: (bi, 0, si)),)
    scratch = [pltpu.VMEM((ATTN_HEADS, LANES), F32), pltpu.VMEM((HALO + tm, POOL_WIDTH), F32)]
    assert len(in_specs) == N_PROJ_IN and len(out_specs) == N_PROJ_OUT
    return in_specs, out_specs, out_shape, scratch


def _mix_specs(d, layer):
    tm = ROW_TILE
    wide = pl.BlockSpec((1, tm, SEG), _row)
    return [pl.BlockSpec((1, tm, d), _row), wide, wide,
            pl.BlockSpec((None, d, d), lambda bi, si: (layer, 0, 0), pipeline_mode=pl.Buffered(1))]


_ROW_PARAMS = pltpu.CompilerParams(dimension_semantics=("arbitrary", "arbitrary"),
                                   vmem_limit_bytes=VMEM_LIMIT)


def _first(x, proj_args):
    b, s, d = x.shape
    in_specs, out_specs, out_shape, scratch = _proj_specs(b, s, d, 0)
    return pl.pallas_call(
        _first_kernel,
        grid=(b, s // ROW_TILE),
        in_specs=[pl.BlockSpec((1, ROW_TILE, d), _row)] + in_specs,
        out_specs=out_specs,
        out_shape=out_shape,
        scratch_shapes=scratch,
        compiler_params=_ROW_PARAMS,
        name="first",
    )(x, *proj_args)


def _middle(x, po, a, w_out, proj_args, layer):
    b, s, d = x.shape
    in_specs, out_specs, out_shape, scratch = _proj_specs(b, s, d, layer + 1)
    return pl.pallas_call(
        _middle_kernel,
        grid=(b, s // ROW_TILE),
        in_specs=_mix_specs(d, layer) + in_specs,
        out_specs=(pl.BlockSpec((1, ROW_TILE, d), _row),) + out_specs,
        out_shape=(jax.ShapeDtypeStruct((b, s, d), F32),) + out_shape,
        scratch_shapes=scratch,
        compiler_params=_ROW_PARAMS,
        name="middle",
    )(x, po, a, w_out, *proj_args)


def _last(x, po, a, w_out, final_g, layer):
    b, s, d = x.shape
    return pl.pallas_call(
        _last_kernel,
        grid=(b, s // ROW_TILE),
        in_specs=_mix_specs(d, layer) + [pl.BlockSpec((1, d), lambda bi, si: (0, 0))],
        out_specs=pl.BlockSpec((1, ROW_TILE, d), _row),
        out_shape=jax.ShapeDtypeStruct((b, s, d), F32),
        compiler_params=_ROW_PARAMS,
        name="last",
    )(x, po, a, w_out, final_g)


def _attn_kernel(q_ref, k_ref, v_ref, c_ref, ag_ref, o_ref, m_sc, l_sc, acc_sc):
    tq, tk, rb = Q_TILE, KV_TILE, ROW_BLOCK
    per = tq // tk
    blocks = tq // rb
    lane = lax.broadcasted_iota(jnp.int32, (1, LANES), 1)
    first = lane < HEAD_DIM

    def load_tile(j):
        start = pl.multiple_of(j * tk, tk)
        vs = v_ref[0, pl.ds(start, tk), :]
        vs1 = jnp.concatenate([vs, jnp.ones_like(vs)], axis=1)
        bias = [c_ref[0, pl.ds(2 * pl.program_id(1) + hh, 1), pl.ds(start, tk)] for hh in range(2)]
        return k_ref[0, pl.ds(start, tk), :], vs1, bias

    def query_tile(qi, carry):
        q_rows = pl.ds(pl.multiple_of(qi * tq, tq), tq)
        q = q_ref[0, q_rows, :]
        zero = jnp.zeros_like(q)
        q0, q1 = jnp.where(first, q, zero), jnp.where(first, zero, q)
        q_units = [jnp.concatenate([q0[r * rb:(r + 1) * rb], q1[r * rb:(r + 1) * rb]], axis=0)
                   for r in range(blocks)]

        m_sc[...] = jnp.full_like(m_sc, NEG_INF)
        l_sc[...] = jnp.zeros_like(l_sc)
        acc_sc[...] = jnp.zeros_like(acc_sc)

        def scores(unit):
            r, n, inputs, _ = unit
            return lax.dot_general(q_units[r], inputs[0][:n], (((1,), (1,)), ((), ())),
                                   preferred_element_type=F32)

        def finish(unit, s2):
            r, n, (_, vs1, bias), col0 = unit
            rows = slice(r * rb, (r + 1) * rb)
            ps, alphas = [], []
            for hh in range(2):
                s = s2[hh * rb:(hh + 1) * rb] - bias[hh][:, :n]
                if col0 is not None:
                    row = r * rb + lax.broadcasted_iota(jnp.int32, (rb, n), 0)
                    col = col0 + lax.broadcasted_iota(jnp.int32, (rb, n), 1)
                    s = jnp.where(row >= col, s, NEG_INF)
                tiles = [s[:, c * LANES:(c + 1) * LANES] for c in range(n // LANES)]
                smax = functools.reduce(jnp.maximum, tiles)
                m_old = m_sc[hh, rows, :]
                m_new = jnp.maximum(m_old, jnp.max(smax, axis=-1, keepdims=True))
                alphas.append(jnp.exp2(m_old - m_new))
                m_sc[hh, rows, :] = m_new
                ps.append(jnp.concatenate([jnp.exp2((x - m_new).astype(BF16)) for x in tiles], axis=1))
            pv = jnp.dot(jnp.concatenate(ps, axis=0), vs1[:n], preferred_element_type=F32)
            for hh in range(2):
                part = pv[hh * rb:(hh + 1) * rb]
                acc_sc[hh, rows, :] = alphas[hh] * acc_sc[hh, rows, :] + part[:, :LANES]
                l_sc[hh, rows, :] = alphas[hh] * l_sc[hh, rows, :] + part[:, LANES:]

        def run(units):
            pending = [scores(u) for u in units[:LOOKAHEAD]]
            for i, unit in enumerate(units):
                if i + LOOKAHEAD < len(units):
                    pending.append(scores(units[i + LOOKAHEAD]))
                finish(unit, pending.pop(0))

        def past_tiles(i, carry):
            units = []
            for d in range(per):
                inputs = load_tile(i * per + d)
                units += [(r, tk, inputs, None) for r in range(blocks)]
            run(units)
            return carry

        lax.fori_loop(0, qi, past_tiles, 0)

        units = []
        for d in range(per):
            inputs = load_tile(qi * per + d)
            for r in range(blocks):
                n = max(0, min(tk, (r + 1) * rb - d * tk))
                if n > 0:
                    causal = r * rb < d * tk + n
                    units.append((r, n, inputs, d * tk if causal else None))
        run(units)

        out = jnp.where(first, acc_sc[0] / l_sc[0], acc_sc[1] / l_sc[1])
        o_ref[0, q_rows, :] = (out * _silu(ag_ref[0, q_rows, :])).astype(BF16)
        return carry

    lax.fori_loop(0, q_ref.shape[1] // tq, query_tile, 0)


def _attn(q, k, v, c, ag):
    b, s, _ = q.shape
    pairs = ATTN_WIDTH // LANES
    full = pl.BlockSpec((1, s, LANES), lambda bi, hp: (bi, 0, hp))
    return pl.pallas_call(
        _attn_kernel,
        grid=(b, pairs),
        in_specs=[full, full, full, pl.BlockSpec((1, ATTN_HEADS, s), lambda bi, hp: (bi, 0, 0)), full],
        out_specs=full,
        out_shape=jax.ShapeDtypeStruct((b, s, ATTN_WIDTH), BF16),
        scratch_shapes=[pltpu.VMEM((2, Q_TILE, LANES), F32)] * 3,
        compiler_params=pltpu.CompilerParams(
            dimension_semantics=("arbitrary", "arbitrary"), vmem_limit_bytes=VMEM_LIMIT),
        name="attn",
    )(q, k, v, c, ag)


def kernel(x, norm_g, w_in, forget_bias, pool_w, pool_scale, w_out, final_g):
    w_all = w_in.astype(BF16)
    w_f = jnp.pad(w_all[:, :, N_SEG * SEG:], ((0, 0), (0, 0), (0, LANES - ATTN_HEADS)))
    fb = forget_bias.astype(F32)[:, :, None]
    g = norm_g.astype(F32)[:, None, :]
    idx = jnp.arange(LANES)
    prefix = jnp.concatenate([(idx[:, None] <= idx[None, :]).astype(BF16),
                              jnp.ones((LANES, LANES), BF16)], axis=1)
    proj_args = (g, w_all, w_f, fb, prefix, pool_w.astype(BF16), pool_scale.astype(F32)[:, None, :])
    wo = w_out.astype(BF16)
    fg = final_g.astype(F32)[None, :]

    po, q, k, v, ag, c = _first(x, proj_args)
    for layer in range(DEPTH):
        a = _attn(q, k, v, c, ag)
        if layer + 1 < DEPTH:
            x, po, q, k, v, ag, c = _middle(x, po, a, wo, proj_args, layer)
        else:
            x = _last(x, po, a, wo, fg, layer)
    return x
```

```python
import functools

import jax
import jax.numpy as jnp
from jax import lax
from jax.experimental import pallas as pl
from jax.experimental.pallas import tpu as pltpu

F32 = jnp.float32
BF16 = jnp.bfloat16

D_MODEL = 1024
DEPTH = 4
POOL_WIDTH = 512
POOL_WINDOWS = (2, 4, 8, 16)
POOL_GROUP_DIM = 128
HEAD_DIM = 64
ATTN_WIDTH = 512
ATTN_HEADS = 8
SEG = 512
N_SEG = 6
RMS_EPS = 1e-6
NEG_INF = -1e30

LANES = 128
HALO = 16
VMEM_LIMIT = 56 * 1024 * 1024

ROW_TILE = 1024
KV_TILE = 512
ROW_BLOCK = 256
LOOKAHEAD = 1
LOG2E = 1.4426950408889634


def _silu(g):
    return g * (1.0 / (1.0 + jnp.exp(-g)))


def _log_sigmoid(z):
    return jnp.minimum(z, 0.0) - jnp.log1p(jnp.exp(-jnp.abs(z)))


def _pool(u, pg, pw_ref, ps_ref, ext_ref):
    si = pl.program_id(1)
    tm = u.shape[0]
    ext_ref[HALO:, :] = u
    pos = si * tm + lax.broadcasted_iota(jnp.int32, (tm, 1), 0)
    parts = []
    for gi, w in enumerate(POOL_WINDOWS):
        cols = slice(gi * POOL_GROUP_DIM, (gi + 1) * POOL_GROUP_DIM)
        win = ext_ref[:, cols]
        for st in range(gi + 1):
            win = win + pltpu.roll(win, 1 << st, axis=0)
        count = jnp.minimum(pos + 1, w).astype(F32)
        dlt = win[HALO:] / count - u[:, cols]
        parts.append(jnp.dot(dlt.astype(BF16), pw_ref[gi], preferred_element_type=F32))
    ext_ref[0:HALO, :] = u[tm - HALO:]
    y = jnp.concatenate(parts, axis=1) * ps_ref[...]
    return (y * _silu(pg)).astype(BF16)


def _project(x, g_ref, w_ref, wf_ref, fb_ref, prefix_ref, pw_ref, ps_ref,
             po_ref, q_ref, k_ref, v_ref, ag_ref, c_ref, carry_ref, ext_ref):
    @pl.when(pl.program_id(1) == 0)
    def _():
        carry_ref[...] = jnp.zeros_like(carry_ref)
        ext_ref[0:HALO, :] = jnp.zeros((HALO, POOL_WIDTH), F32)

    ms = jnp.mean(x * x, axis=-1, keepdims=True)
    h = (x * lax.rsqrt(ms + RMS_EPS) * g_ref[...]).astype(BF16)

    def seg(i):
        return jnp.dot(h, w_ref[:, i * SEG:(i + 1) * SEG], preferred_element_type=F32)

    z = jnp.dot(h, wf_ref[...], preferred_element_type=F32)
    po_ref[0] = _pool(seg(0), seg(1), pw_ref, ps_ref, ext_ref)
    q_ref[0] = (seg(2) * (HEAD_DIM ** -0.5 * LOG2E)).astype(BF16)
    k_ref[0] = seg(3).astype(BF16)

    lf = _log_sigmoid(z.T[:ATTN_HEADS] + fb_ref[...])
    tm = lf.shape[1]
    hi = lf.astype(BF16).astype(F32)
    rest = lf - hi
    mid = rest.astype(BF16).astype(F32)
    parts = jnp.concatenate([hi, mid, rest - mid, jnp.zeros_like(lf)], axis=0).astype(BF16)
    carry = carry_ref[...]
    for j in range(tm // LANES):
        res = jnp.dot(parts[:, j * LANES:(j + 1) * LANES], prefix_ref[...], preferred_element_type=F32)
        res = res[0:8] + res[8:16] + res[16:24]
        c_ref[0, :, j * LANES:(j + 1) * LANES] = (res[:, :LANES] + carry) * LOG2E
        carry = carry + res[:, LANES:]
    carry_ref[...] = carry

    v_ref[0] = seg(4).astype(BF16)
    ag_ref[0] = seg(5)


def _mix(x_ref, po_ref, a_ref, wo_ref):
    acc = jnp.dot(po_ref[0], wo_ref[0:POOL_WIDTH, :], preferred_element_type=F32)
    acc = acc + jnp.dot(a_ref[0], wo_ref[POOL_WIDTH:, :], preferred_element_type=F32)
    return x_ref[0] + acc


N_PROJ_IN = 7
N_PROJ_OUT = 6


def _first_kernel(x_ref, *refs):
    _project(x_ref[0], *refs)


def _middle_kernel(x_ref, po_ref, a_ref, wo_ref, *refs):
    proj_in, (xo_ref, *proj_out) = refs[:N_PROJ_IN], refs[N_PROJ_IN:N_PROJ_IN + 1 + N_PROJ_OUT]
    scratch = refs[N_PROJ_IN + 1 + N_PROJ_OUT:]
    xn = _mix(x_ref, po_ref, a_ref, wo_ref)
    xo_ref[0] = xn
    _project(xn, *proj_in, *proj_out, *scratch)


def _last_kernel(x_ref, po_ref, a_ref, wo_ref, fg_ref, o_ref):
    xn = _mix(x_ref, po_ref, a_ref, wo_ref)
    ms = jnp.mean(xn * xn, axis=-1, keepdims=True)
    o_ref[0] = xn * lax.rsqrt(ms + RMS_EPS) * fg_ref[...]


def _row(bi, si):
    return (bi, si, 0)


def _proj_specs(b, s, d, layer):
    tm = ROW_TILE
    pick = lambda bi, si: (layer, 0, 0)
    wide = N_SEG * SEG
    in_specs = [
        pl.BlockSpec((None, 1, d), pick),
        pl.BlockSpec((None, d, wide), pick, pipeline_mode=pl.Buffered(1)),
        pl.BlockSpec((None, d, LANES), pick),
        pl.BlockSpec((None, ATTN_HEADS, 1), pick),
        pl.BlockSpec((LANES, 2 * LANES), lambda bi, si: (0, 0)),
        pl.BlockSpec((None, len(POOL_WINDOWS), POOL_GROUP_DIM, POOL_GROUP_DIM),
                     lambda bi, si: (layer, 0, 0, 0)),
        pl.BlockSpec((None, 1, POOL_WIDTH), pick),
    ]
    wide_f32 = jax.ShapeDtypeStruct((b, s, SEG), F32)
    wide_bf16 = jax.ShapeDtypeStruct((b, s, SEG), BF16)
    out_shape = (wide_bf16, wide_bf16, wide_bf16, wide_bf16, wide_f32,
                 jax.ShapeDtypeStruct((b, ATTN_HEADS, s), F32))
    out_specs = (pl.BlockSpec((1, tm, SEG), _row),) * 5 + (
        pl.BlockSpec((1, ATTN_HEADS, tm), lambda bi, si: (bi, 0, si)),)
    scratch = [pltpu.VMEM((ATTN_HEADS, LANES), F32), pltpu.VMEM((HALO + tm, POOL_WIDTH), F32)]
    assert len(in_specs) == N_PROJ_IN and len(out_specs) == N_PROJ_OUT
    return in_specs, out_specs, out_shape, scratch


def _mix_specs(d, layer):
    tm = ROW_TILE
    wide = pl.BlockSpec((1, tm, SEG), _row)
    return [pl.BlockSpec((1, tm, d), _row), wide, wide,
            pl.BlockSpec((None, d, d), lambda bi, si: (layer, 0, 0), pipeline_mode=pl.Buffered(1))]


_ROW_PARAMS = pltpu.CompilerParams(dimension_semantics=("arbitrary", "arbitrary"),
                                   vmem_limit_bytes=VMEM_LIMIT)


def _first(x, proj_args):
    b, s, d = x.shape
    in_specs, out_specs, out_shape, scratch = _proj_specs(b, s, d, 0)
    return pl.pallas_call(
        _first_kernel,
        grid=(b, s // ROW_TILE),
        in_specs=[pl.BlockSpec((1, ROW_TILE, d), _row)] + in_specs,
        out_specs=out_specs,
        out_shape=out_shape,
        scratch_shapes=scratch,
        compiler_params=_ROW_PARAMS,
        name="first",
    )(x, *proj_args)


def _middle(x, po, a, w_out, proj_args, layer):
    b, s, d = x.shape
    in_specs, out_specs, out_shape, scratch = _proj_specs(b, s, d, layer + 1)
    return pl.pallas_call(
        _middle_kernel,
        grid=(b, s // ROW_TILE),
        in_specs=_mix_specs(d, layer) + in_specs,
        out_specs=(pl.BlockSpec((1, ROW_TILE, d), _row),) + out_specs,
        out_shape=(jax.ShapeDtypeStruct((b, s, d), F32),) + out_shape,
        scratch_shapes=scratch,
        compiler_params=_ROW_PARAMS,
        name="middle",
    )(x, po, a, w_out, *proj_args)


def _last(x, po, a, w_out, final_g, layer):
    b, s, d = x.shape
    return pl.pallas_call(
        _last_kernel,
        grid=(b, s // ROW_TILE),
        in_specs=_mix_specs(d, layer) + [pl.BlockSpec((1, d), lambda bi, si: (0, 0))],
        out_specs=pl.BlockSpec((1, ROW_TILE, d), _row),
        out_shape=jax.ShapeDtypeStruct((b, s, d), F32),
        compiler_params=_ROW_PARAMS,
        name="last",
    )(x, po, a, w_out, final_g)


def _attn_kernel(q_ref, k_ref, v_ref, c_ref, ag_ref, o_ref, m_sc, l_sc, acc_sc):
    tk, rb = KV_TILE, ROW_BLOCK
    s_len = q_ref.shape[1]
    lane = lax.broadcasted_iota(jnp.int32, (1, LANES), 1)
    first = lane < HEAD_DIM
    q = q_ref[0]
    zero = jnp.zeros_like(q)
    q0, q1 = jnp.where(first, q, zero), jnp.where(first, zero, q)
    q_units = [jnp.concatenate([q0[r * rb:(r + 1) * rb], q1[r * rb:(r + 1) * rb]], axis=0)
               for r in range(s_len // rb)]

    m_sc[...] = jnp.full_like(m_sc, NEG_INF)
    l_sc[...] = jnp.zeros_like(l_sc)
    acc_sc[...] = jnp.zeros_like(acc_sc)

    def load_tile(j):
        keys = slice(j * tk, (j + 1) * tk)
        vs = v_ref[0, keys, :]
        vs1 = jnp.concatenate([vs, jnp.ones_like(vs)], axis=1)
        bias = [c_ref[0, pl.ds(2 * pl.program_id(1) + hh, 1), keys] for hh in range(2)]
        return k_ref[0, keys, :], vs1, bias

    def scores(unit):
        r, n, inputs, _ = unit
        return lax.dot_general(q_units[r], inputs[0][:n], (((1,), (1,)), ((), ())),
                               preferred_element_type=F32)

    def finish(unit, s2):
        r, n, (_, vs1, bias), col0 = unit
        rows = slice(r * rb, (r + 1) * rb)
        ps, alphas = [], []
        for hh in range(2):
            s = s2[hh * rb:(hh + 1) * rb] - bias[hh][:, :n]
            if col0 is not None:
                row = r * rb + lax.broadcasted_iota(jnp.int32, (rb, n), 0)
                col = col0 + lax.broadcasted_iota(jnp.int32, (rb, n), 1)
                s = jnp.where(row >= col, s, NEG_INF)
            tiles = [s[:, c * LANES:(c + 1) * LANES] for c in range(n // LANES)]
            smax = functools.reduce(jnp.maximum, tiles)
            m_old = m_sc[hh, rows, :]
            m_new = jnp.maximum(m_old, jnp.max(smax, axis=-1, keepdims=True))
            alphas.append(jnp.exp2(m_old - m_new))
            m_sc[hh, rows, :] = m_new
            ps.append(jnp.concatenate([jnp.exp2((x - m_new).astype(BF16)) for x in tiles], axis=1))
        pv = jnp.dot(jnp.concatenate(ps, axis=0), vs1[:n], preferred_element_type=F32)
        for hh in range(2):
            part = pv[hh * rb:(hh + 1) * rb]
            acc_sc[hh, rows, :] = alphas[hh] * acc_sc[hh, rows, :] + part[:, :LANES]
            l_sc[hh, rows, :] = alphas[hh] * l_sc[hh, rows, :] + part[:, LANES:]

    units = []
    for j in range(s_len // tk):
        inputs = load_tile(j)
        for r in range(s_len // rb):
            n = max(0, min(tk, (r + 1) * rb - j * tk))
            if n > 0:
                causal = r * rb < j * tk + n
                units.append((r, n, inputs, j * tk if causal else None))

    pending = [scores(u) for u in units[:LOOKAHEAD]]
    for i, unit in enumerate(units):
        if i + LOOKAHEAD < len(units):
            pending.append(scores(units[i + LOOKAHEAD]))
        finish(unit, pending.pop(0))

    out = jnp.where(first, acc_sc[0] / l_sc[0], acc_sc[1] / l_sc[1])
    o_ref[0] = (out * _silu(ag_ref[0])).astype(BF16)


def _attn(q, k, v, c, ag):
    b, s, _ = q.shape
    pairs = ATTN_WIDTH // LANES
    full = pl.BlockSpec((1, s, LANES), lambda bi, hp: (bi, 0, hp))
    return pl.pallas_call(
        _attn_kernel,
        grid=(b, pairs),
        in_specs=[full, full, full, pl.BlockSpec((1, ATTN_HEADS, s), lambda bi, hp: (bi, 0, 0)), full],
        out_specs=full,
        out_shape=jax.ShapeDtypeStruct((b, s, ATTN_WIDTH), BF16),
        scratch_shapes=[pltpu.VMEM((2, s, LANES), F32)] * 3,
        compiler_params=pltpu.CompilerParams(
            dimension_semantics=("arbitrary", "arbitrary"), vmem_limit_bytes=VMEM_LIMIT),
        name="attn",
    )(q, k, v, c, ag)


def kernel(x, norm_g, w_in, forget_bias, pool_w, pool_scale, w_out, final_g):
    w_all = w_in.astype(BF16)
    w_f = jnp.pad(w_all[:, :, N_SEG * SEG:], ((0, 0), (0, 0), (0, LANES - ATTN_HEADS)))
    fb = forget_bias.astype(F32)[:, :, None]
    g = norm_g.astype(F32)[:, None, :]
    idx = jnp.arange(LANES)
    prefix = jnp.concatenate([(idx[:, None] <= idx[None, :]).astype(BF16),
                              jnp.ones((LANES, LANES), BF16)], axis=1)
    proj_args = (g, w_all, w_f, fb, prefix, pool_w.astype(BF16), pool_scale.astype(F32)[:, None, :])
    wo = w_out.astype(BF16)
    fg = final_g.astype(F32)[None, :]

    po, q, k, v, ag, c = _first(x, proj_args)
    for layer in range(DEPTH):
        a = _attn(q, k, v, c, ag)
        if layer + 1 < DEPTH:
            x, po, q, k, v, ag, c = _middle(x, po, a, wo, proj_args, layer)
        else:
            x = _last(x, po, a, wo, fg, layer)
    return x
```

```python
import functools

import jax
import jax.numpy as jnp
from jax import lax
from jax.experimental import pallas as pl
from jax.experimental.pallas import tpu as pltpu

F32 = jnp.float32
BF16 = jnp.bfloat16

D_MODEL = 1024
DEPTH = 4
POOL_WIDTH = 512
POOL_WINDOWS = (2, 4, 8, 16)
POOL_GROUP_DIM = 128
HEAD_DIM = 64
ATTN_WIDTH = 512
ATTN_HEADS = 8
SEG = 512
N_SEG = 6
RMS_EPS = 1e-6
NEG_INF = -1e30

LANES = 128
HALO = 16
VMEM_LIMIT = 56 * 1024 * 1024

ROW_TILE = 1024
KV_TILE = 512
ROW_BLOCK = 256
LOOKAHEAD = 1
LOG2E = 1.4426950408889634


def _silu(g):
    return g * (1.0 / (1.0 + jnp.exp(-g)))


def _log_sigmoid(z):
    return jnp.minimum(z, 0.0) - jnp.log1p(jnp.exp(-jnp.abs(z)))


def _pool(u, pg, pw_ref, ps_ref, ext_ref):
    si = pl.program_id(1)
    tm = u.shape[0]
    ext_ref[HALO:, :] = u
    pos = si * tm + lax.broadcasted_iota(jnp.int32, (tm, 1), 0)
    parts = []
    for gi, w in enumerate(POOL_WINDOWS):
        cols = slice(gi * POOL_GROUP_DIM, (gi + 1) * POOL_GROUP_DIM)
        win = ext_ref[:, cols]
        for st in range(gi + 1):
            win = win + pltpu.roll(win, 1 << st, axis=0)
        count = jnp.minimum(pos + 1, w).astype(F32)
        dlt = win[HALO:] / count - u[:, cols]
        parts.append(jnp.dot(dlt.astype(BF16), pw_ref[gi], preferred_element_type=F32))
    ext_ref[0:HALO, :] = u[tm - HALO:]
    y = jnp.concatenate(parts, axis=1) * ps_ref[...]
    return (y * _silu(pg)).astype(BF16)


def _project(x, g_ref, w_ref, wf_ref, fb_ref, prefix_ref, pw_ref, ps_ref,
             po_ref, q_ref, k_ref, v_ref, ag_ref, c_ref, carry_ref, ext_ref):
    @pl.when(pl.program_id(1) == 0)
    def _():
        carry_ref[...] = jnp.zeros_like(carry_ref)
        ext_ref[0:HALO, :] = jnp.zeros((HALO, POOL_WIDTH), F32)

    ms = jnp.mean(x * x, axis=-1, keepdims=True)
    h = (x * lax.rsqrt(ms + RMS_EPS) * g_ref[...]).astype(BF16)

    half = N_SEG // 2
    wide = {}

    def seg(i):
        grp = i // half
        if grp not in wide:
            wide[grp] = jnp.dot(h, w_ref[:, grp * half * SEG:(grp + 1) * half * SEG],
                                preferred_element_type=F32)
        return wide[grp][:, (i % half) * SEG:(i % half + 1) * SEG]

    z = jnp.dot(h, wf_ref[...], preferred_element_type=F32)
    po_ref[0] = _pool(seg(0), seg(1), pw_ref, ps_ref, ext_ref)
    q_ref[0] = (seg(2) * (HEAD_DIM ** -0.5 * LOG2E)).astype(BF16)
    k_ref[0] = seg(3).astype(BF16)

    lf = _log_sigmoid(z.T[:ATTN_HEADS] + fb_ref[...])
    tm = lf.shape[1]
    hi = lf.astype(BF16).astype(F32)
    rest = lf - hi
    mid = rest.astype(BF16).astype(F32)
    parts = jnp.concatenate([hi, mid, rest - mid, jnp.zeros_like(lf)], axis=0).astype(BF16)
    carry = carry_ref[...]
    for j in range(tm // LANES):
        res = jnp.dot(parts[:, j * LANES:(j + 1) * LANES], prefix_ref[...], preferred_element_type=F32)
        res = res[0:ATTN_HEADS] + res[ATTN_HEADS:2 * ATTN_HEADS] + res[2 * ATTN_HEADS:3 * ATTN_HEADS]
        c_ref[0, :, j * LANES:(j + 1) * LANES] = (res[:, :LANES] + carry) * LOG2E
        carry = carry + res[:, LANES:]
    carry_ref[...] = carry

    v_ref[0] = seg(4).astype(BF16)
    ag_ref[0] = seg(5)


def _mix(x_ref, po_ref, a_ref, wo_ref):
    acc = jnp.dot(po_ref[0], wo_ref[0:POOL_WIDTH, :], preferred_element_type=F32)
    acc = acc + jnp.dot(a_ref[0], wo_ref[POOL_WIDTH:, :], preferred_element_type=F32)
    return x_ref[0] + acc


N_PROJ_IN = 7
N_PROJ_OUT = 6


def _first_kernel(x_ref, *refs):
    _project(x_ref[0], *refs)


def _middle_kernel(x_ref, po_ref, a_ref, wo_ref, *refs):
    proj_in, (xo_ref, *proj_out) = refs[:N_PROJ_IN], refs[N_PROJ_IN:N_PROJ_IN + 1 + N_PROJ_OUT]
    scratch = refs[N_PROJ_IN + 1 + N_PROJ_OUT:]
    xn = _mix(x_ref, po_ref, a_ref, wo_ref)
    xo_ref[0] = xn
    _project(xn, *proj_in, *proj_out, *scratch)


def _last_kernel(x_ref, po_ref, a_ref, wo_ref, fg_ref, o_ref):
    xn = _mix(x_ref, po_ref, a_ref, wo_ref)
    ms = jnp.mean(xn * xn, axis=-1, keepdims=True)
    o_ref[0] = xn * lax.rsqrt(ms + RMS_EPS) * fg_ref[...]


def _row(bi, si):
    return (bi, si, 0)


def _proj_specs(b, s, d, layer):
    tm = ROW_TILE
    pick = lambda bi, si: (layer, 0, 0)
    wide = N_SEG * SEG
    in_specs = [
        pl.BlockSpec((None, 1, d), pick),
        pl.BlockSpec((None, d, wide), pick, pipeline_mode=pl.Buffered(1)),
        pl.BlockSpec((None, d, LANES), pick),
        pl.BlockSpec((None, ATTN_HEADS, 1), pick),
        pl.BlockSpec((LANES, 2 * LANES), lambda bi, si: (0, 0)),
        pl.BlockSpec((None, len(POOL_WINDOWS), POOL_GROUP_DIM, POOL_GROUP_DIM),
                     lambda bi, si: (layer, 0, 0, 0)),
        pl.BlockSpec((None, 1, POOL_WIDTH), pick),
    ]
    wide_f32 = jax.ShapeDtypeStruct((b, s, SEG), F32)
    wide_bf16 = jax.ShapeDtypeStruct((b, s, SEG), BF16)
    out_shape = (wide_bf16, wide_bf16, wide_bf16, wide_bf16, wide_f32,
                 jax.ShapeDtypeStruct((b, ATTN_HEADS, s), F32))
    out_specs = (pl.BlockSpec((1, tm, SEG), _row),) * 5 + (
        pl.BlockSpec((1, ATTN_HEADS, tm), lambda bi, si: (bi, 0, si)),)
    scratch = [pltpu.VMEM((ATTN_HEADS, LANES), F32), pltpu.VMEM((HALO + tm, POOL_WIDTH), F32)]
    assert len(in_specs) == N_PROJ_IN and len(out_specs) == N_PROJ_OUT
    return in_specs, out_specs, out_shape, scratch


def _mix_specs(d, layer):
    tm = ROW_TILE
    wide = pl.BlockSpec((1, tm, SEG), _row)
    return [pl.BlockSpec((1, tm, d), _row), wide, wide,
            pl.BlockSpec((None, d, d), lambda bi, si: (layer, 0, 0), pipeline_mode=pl.Buffered(1))]


_ROW_PARAMS = pltpu.CompilerParams(dimension_semantics=("arbitrary", "arbitrary"),
                                   vmem_limit_bytes=VMEM_LIMIT)


def _first(x, proj_args):
    b, s, d = x.shape
    in_specs, out_specs, out_shape, scratch = _proj_specs(b, s, d, 0)
    return pl.pallas_call(
        _first_kernel,
        grid=(b, s // ROW_TILE),
        in_specs=[pl.BlockSpec((1, ROW_TILE, d), _row)] + in_specs,
        out_specs=out_specs,
        out_shape=out_shape,
        scratch_shapes=scratch,
        compiler_params=_ROW_PARAMS,
        name="first",
    )(x, *proj_args)


def _middle(x, po, a, w_out, proj_args, layer):
    b, s, d = x.shape
    in_specs, out_specs, out_shape, scratch = _proj_specs(b, s, d, layer + 1)
    return pl.pallas_call(
        _middle_kernel,
        grid=(b, s // ROW_TILE),
        in_specs=_mix_specs(d, layer) + in_specs,
        out_specs=(pl.BlockSpec((1, ROW_TILE, d), _row),) + out_specs,
        out_shape=(jax.ShapeDtypeStruct((b, s, d), F32),) + out_shape,
        scratch_shapes=scratch,
        compiler_params=_ROW_PARAMS,
        name="middle",
    )(x, po, a, w_out, *proj_args)


def _last(x, po, a, w_out, final_g, layer):
    b, s, d = x.shape
    return pl.pallas_call(
        _last_kernel,
        grid=(b, s // ROW_TILE),
        in_specs=_mix_specs(d, layer) + [pl.BlockSpec((1, d), lambda bi, si: (0, 0))],
        out_specs=pl.BlockSpec((1, ROW_TILE, d), _row),
        out_shape=jax.ShapeDtypeStruct((b, s, d), F32),
        compiler_params=_ROW_PARAMS,
        name="last",
    )(x, po, a, w_out, final_g)


def _attn_kernel(q_ref, k_ref, v_ref, c_ref, ag_ref, o_ref, m_sc, l_sc, acc_sc):
    tk, rb = KV_TILE, ROW_BLOCK
    s_len = q_ref.shape[1]
    lane = lax.broadcasted_iota(jnp.int32, (1, LANES), 1)
    first = lane < HEAD_DIM
    q = q_ref[0]
    zero = jnp.zeros_like(q)
    q0, q1 = jnp.where(first, q, zero), jnp.where(first, zero, q)
    q_units = [jnp.concatenate([q0[r * rb:(r + 1) * rb], q1[r * rb:(r + 1) * rb]], axis=0)
               for r in range(s_len // rb)]

    m_sc[...] = jnp.full_like(m_sc, NEG_INF)
    l_sc[...] = jnp.zeros_like(l_sc)
    acc_sc[...] = jnp.zeros_like(acc_sc)

    def load_tile(j):
        keys = slice(j * tk, (j + 1) * tk)
        vs = v_ref[0, keys, :]
        vs1 = jnp.concatenate([vs, jnp.ones_like(vs)], axis=1)
        bias = [c_ref[0, pl.ds(2 * pl.program_id(1) + hh, 1), keys] for hh in range(2)]
        return k_ref[0, keys, :], vs1, bias

    def scores(unit):
        r, n, inputs, _ = unit
        return lax.dot_general(q_units[r], inputs[0][:n], (((1,), (1,)), ((), ())),
                               preferred_element_type=F32)

    def finish(unit, s2):
        r, n, (_, vs1, bias), col0 = unit
        rows = slice(r * rb, (r + 1) * rb)
        ps, alphas = [], []
        for hh in range(2):
            s = s2[hh * rb:(hh + 1) * rb] - bias[hh][:, :n]
            if col0 is not None:
                row = r * rb + lax.broadcasted_iota(jnp.int32, (rb, n), 0)
                col = col0 + lax.broadcasted_iota(jnp.int32, (rb, n), 1)
                s = jnp.where(row >= col, s, NEG_INF)
            tiles = [s[:, c * LANES:(c + 1) * LANES] for c in range(n // LANES)]
            smax = functools.reduce(jnp.maximum, tiles)
            m_old = m_sc[hh, rows, :]
            m_new = jnp.maximum(m_old, jnp.max(smax, axis=-1, keepdims=True))
            alphas.append(jnp.exp2(m_old - m_new))
            m_sc[hh, rows, :] = m_new
            ps.append(jnp.concatenate([jnp.exp2((x - m_new).astype(BF16)) for x in tiles], axis=1))
        pv = jnp.dot(jnp.concatenate(ps, axis=0), vs1[:n], preferred_element_type=F32)
        for hh in range(2):
            part = pv[hh * rb:(hh + 1) * rb]
            acc_sc[hh, rows, :] = alphas[hh] * acc_sc[hh, rows, :] + part[:, :LANES]
            l_sc[hh, rows, :] = alphas[hh] * l_sc[hh, rows, :] + part[:, LANES:]

    units = []
    for j in range(s_len // tk):
        inputs = load_tile(j)
        for r in range(s_len // rb):
            n = max(0, min(tk, (r + 1) * rb - j * tk))
            if n > 0:
                causal = r * rb < j * tk + n
                units.append((r, n, inputs, j * tk if causal else None))

    pending = [scores(u) for u in units[:LOOKAHEAD]]
    for i, unit in enumerate(units):
        if i + LOOKAHEAD < len(units):
            pending.append(scores(units[i + LOOKAHEAD]))
        finish(unit, pending.pop(0))

    out = jnp.where(first, acc_sc[0] / l_sc[0], acc_sc[1] / l_sc[1])
    o_ref[0] = (out * _silu(ag_ref[0])).astype(BF16)


def _attn(q, k, v, c, ag):
    b, s, _ = q.shape
    pairs = ATTN_WIDTH // LANES
    full = pl.BlockSpec((1, s, LANES), lambda bi, hp: (bi, 0, hp))
    return pl.pallas_call(
        _attn_kernel,
        grid=(b, pairs),
        in_specs=[full, full, full, pl.BlockSpec((1, ATTN_HEADS, s), lambda bi, hp: (bi, 0, 0)), full],
        out_specs=full,
        out_shape=jax.ShapeDtypeStruct((b, s, ATTN_WIDTH), BF16),
        scratch_shapes=[pltpu.VMEM((2, s, LANES), F32)] * 3,
        compiler_params=pltpu.CompilerParams(
            dimension_semantics=("arbitrary", "arbitrary"), vmem_limit_bytes=VMEM_LIMIT),
        name="attn",
    )(q, k, v, c, ag)


def kernel(x, norm_g, w_in, forget_bias, pool_w, pool_scale, w_out, final_g):
    w_all = w_in.astype(BF16)
    w_f = jnp.pad(w_all[:, :, N_SEG * SEG:], ((0, 0), (0, 0), (0, LANES - ATTN_HEADS)))
    fb = forget_bias.astype(F32)[:, :, None]
    g = norm_g.astype(F32)[:, None, :]
    idx = jnp.arange(LANES)
    prefix = jnp.concatenate([(idx[:, None] <= idx[None, :]).astype(BF16),
                              jnp.ones((LANES, LANES), BF16)], axis=1)
    proj_args = (g, w_all, w_f, fb, prefix, pool_w.astype(BF16), pool_scale.astype(F32)[:, None, :])
    wo = w_out.astype(BF16)
    fg = final_g.astype(F32)[None, :]

    po, q, k, v, ag, c = _first(x, proj_args)
    for layer in range(DEPTH):
        a = _attn(q, k, v, c, ag)
        if layer + 1 < DEPTH:
            x, po, q, k, v, ag, c = _middle(x, po, a, wo, proj_args, layer)
        else:
            x = _last(x, po, a, wo, fg, layer)
    return x
```
